```python
import jax, jax.numpy as jnp
from jax import lax
import numpy as np

D_MODEL = 1024
BATCH = 32
SEQ = 256
DEPTH = 4
DEC_BATCH = 4
DEC_SEQ = 1024
PAST_LEN = 256

GRID_W = 64
N_MIXERS = 3
N_NA_LAYERS = (DEPTH + 2) // 3
N_RET_LAYERS = (DEPTH + 1) // 3
N_MLP_LAYERS = DEPTH // 3
EPS = 1e-6
NEG_INF = -1e30

NA_HEADS = 16
NA_HEAD_DIM = D_MODEL // NA_HEADS
NA_WIDTH = NA_HEADS * NA_HEAD_DIM
NA_KH = 8
NA_KW = 16
ATTN_Q_BLOCK = 128

RET_HEADS = 4
RET_QK_DIM = D_MODEL // RET_HEADS
RET_V_DIM = 2 * D_MODEL // RET_HEADS
RET_QK_WIDTH = RET_HEADS * RET_QK_DIM
RET_WIDTH = RET_HEADS * RET_V_DIM
RET_CHUNK = 128
ROPE_BASE = 10000.0

MLP_WIDTH = 2 * D_MODEL
MLP_GROUPS = 8
MLP_GROUP_DIM = MLP_WIDTH // MLP_GROUPS
MLP_CHUNK = 128

kernel_name = "hybrid_na_retention_gmlp_diffusion_step"


def rms_norm(x, w):
    xf = x.astype(jnp.float32)
    y = xf * lax.rsqrt(jnp.mean(xf * xf, axis=-1, keepdims=True) + EPS)
    return (y * w.astype(jnp.float32)).astype(x.dtype)


def layer_norm(x, w, b):
    xf = x.astype(jnp.float32)
    mu = jnp.mean(xf, axis=-1, keepdims=True)
    var = jnp.mean(jnp.square(xf - mu), axis=-1, keepdims=True)
    y = (xf - mu) * lax.rsqrt(var + EPS)
    return (y * w.astype(jnp.float32) + b.astype(jnp.float32)).astype(x.dtype)


def adaln(cvec, w, b):
    m = jax.nn.silu(cvec) @ w + b
    return jnp.split(m, 3, axis=-1)


def axial_rope(x):
    N, Dh = x.shape[1], x.shape[-1]
    half = Dh // 2
    t = jnp.arange(N)
    row = (t // GRID_W).astype(jnp.float32)
    col = (t % GRID_W).astype(jnp.float32)

    def rot(xa, pos):
        d = xa.shape[-1]
        inv = ROPE_BASE ** (-jnp.arange(0, d, 2, dtype=jnp.float32) / d)
        ang = pos[:, None] * inv[None, :]
        cos = jnp.cos(ang)[None, :, None, :]
        sin = jnp.sin(ang)[None, :, None, :]
        x1, x2 = xa[..., : d // 2], xa[..., d // 2:]
        return jnp.concatenate([x1 * cos - x2 * sin, x1 * sin + x2 * cos], axis=-1).astype(xa.dtype)

    return jnp.concatenate([rot(x[..., :half], row), rot(x[..., half:], col)], axis=-1)


def na_project(h, w_in, q_gain, k_gain):
    B, N, _ = h.shape
    q, k, v, g = jnp.split(h @ w_in, 4, axis=-1)
    q = rms_norm(q.reshape(B, N, NA_HEADS, NA_HEAD_DIM), q_gain)
    k = rms_norm(k.reshape(B, N, NA_HEADS, NA_HEAD_DIM), k_gain)
    v = v.reshape(B, N, NA_HEADS, NA_HEAD_DIM)
    return q, k, v, g


def dense_attention(q, k, v):
    B, S, H, Dh = q.shape
    scale = Dh ** -0.5
    qb = q.reshape(B, S // ATTN_Q_BLOCK, ATTN_Q_BLOCK, H, Dh).transpose(1, 0, 2, 3, 4)

    def blk(qi):
        s = jnp.einsum('bqhd,bkhd->bhqk', qi, k).astype(jnp.float32) * scale
        p = jax.nn.softmax(s, axis=-1).astype(v.dtype)
        return jnp.einsum('bhqk,bkhd->bqhd', p, v)

    o = lax.map(blk, qb)
    return o.transpose(1, 0, 2, 3, 4).reshape(B, S, H, Dh)


def na_context(h, w_in, w_out, q_gain, k_gain):
    B, S, _ = h.shape
    q, k, v, g = na_project(h, w_in, q_gain, k_gain)
    o = dense_attention(q, k, v).reshape(B, S, NA_WIDTH) * jax.nn.silu(g)
    return o @ w_out, k, v


def na_latent(h, k_ctx, v_ctx, w_in, w_out, q_gain, k_gain, rpb):
    B, N, _ = h.shape
    R = N // GRID_W
    kh = min(NA_KH, R)
    scale = NA_HEAD_DIM ** -0.5
    q, k, v, g = na_project(h, w_in, q_gain, k_gain)
    qg = q.reshape(B, R, GRID_W, NA_HEADS, NA_HEAD_DIM)
    kg = k.reshape(B, R, GRID_W, NA_HEADS, NA_HEAD_DIM)
    vg = v.reshape(B, R, GRID_W, NA_HEADS, NA_HEAD_DIM)

    r = np.arange(R)
    rstart = np.clip(r - kh // 2, 0, R - kh)
    row_idx = rstart[:, None] + np.arange(kh)[None, :]
    cq = np.arange(GRID_W)
    cstart = np.clip(cq - NA_KW // 2, 0, GRID_W - NA_KW)
    ck = np.arange(GRID_W)
    col_in = (ck[None, :] >= cstart[:, None]) & (ck[None, :] < cstart[:, None] + NA_KW)

    k_rows = kg[:, row_idx]
    v_rows = vg[:, row_idx]
    s_loc = jnp.einsum('brqhd,brjkhd->brhqjk', qg, k_rows).astype(jnp.float32) * scale

    roff = row_idx - r[:, None] + (NA_KH - 1)
    coff = np.clip(ck[None, :] - cq[:, None], -(NA_KW - 1), NA_KW - 1) + (NA_KW - 1)
    bias = rpb[:, roff[:, None, :, None], coff[None, :, None, :]]
    bias = bias.transpose(1, 0, 2, 3, 4).astype(jnp.float32)
    s_loc = jnp.where(col_in[:, None, :], s_loc + bias[None], NEG_INF)
    s_loc = s_loc.reshape(B, R, NA_HEADS, GRID_W, kh * GRID_W)

    s_ctx = jnp.einsum('brqhd,bkhd->brhqk', qg, k_ctx).astype(jnp.float32) * scale
    p = jax.nn.softmax(jnp.concatenate([s_loc, s_ctx], axis=-1), axis=-1).astype(v.dtype)
    p_loc = p[..., : kh * GRID_W].reshape(B, R, NA_HEADS, GRID_W, kh, GRID_W)
    p_ctx = p[..., kh * GRID_W:]
    o = (jnp.einsum('brhqjk,brjkhd->brqhd', p_loc, v_rows)
         + jnp.einsum('brhqk,bkhd->brqhd', p_ctx, v_ctx))
    o = o.reshape(B, N, NA_WIDTH) * jax.nn.silu(g)
    return o @ w_out


def retention_scan(q, k, v, log_gamma, s0):
    B, N, H, Dk = q.shape
    Dv = v.shape[-1]
    C = RET_CHUNK
    nc = N // C
    lg = log_gamma.astype(jnp.float32)
    pos = jnp.arange(C, dtype=jnp.float32)
    decay_q = jnp.exp(lg[None, :] * (pos[:, None] + 1.0))[None, :, :, None]
    decay_k = jnp.exp(lg[None, :] * (C - 1.0 - pos[:, None]))[None, :, :, None]
    diff = pos[:, None] - pos[None, :]
    dmat = jnp.where(diff[None] >= 0, jnp.exp(lg[:, None, None] * jnp.maximum(diff, 0.0)[None]), 0.0)
    chunk_decay = jnp.exp(lg * C)[None, :, None, None]
    kscale = Dk ** -0.5

    def to_chunks(a):
        return a.reshape(B, nc, C, H, a.shape[-1]).transpose(1, 0, 2, 3, 4)

    def step(s, xs):
        qc, kc, vc = xs
        qf = qc.astype(jnp.float32)
        kf = kc.astype(jnp.float32) * kscale
        vf = vc.astype(jnp.float32)
        inner = jnp.einsum('bihd,bjhd->bhij', qf, kf) * dmat[None]
        o = (jnp.einsum('bhij,bjhv->bihv', inner, vf)
             + jnp.einsum('bihd,bhdv->bihv', qf, s) * decay_q)
        s = s * chunk_decay + jnp.einsum('bjhd,bjhv->bhdv', kf * decay_k, vf)
        return s, o

    s_final, o = lax.scan(step, s0.astype(jnp.float32), (to_chunks(q), to_chunks(k), to_chunks(v)))
    o = o.transpose(1, 0, 2, 3, 4).reshape(B, N, H, Dv)
    return o, s_final


def ret_project(h, w_in):
    B, N, _ = h.shape
    q, k, v, g = jnp.split(h @ w_in, [RET_QK_WIDTH, 2 * RET_QK_WIDTH, 2 * RET_QK_WIDTH + RET_WIDTH], axis=-1)
    q = q.reshape(B, N, RET_HEADS, RET_QK_DIM)
    k = k.reshape(B, N, RET_HEADS, RET_QK_DIM)
    v = v.reshape(B, N, RET_HEADS, RET_V_DIM)
    return q, k, v, g


def ret_bidirectional(q, k, v, decay_logit, s0_f, s0_b):
    lg = jax.nn.log_sigmoid(decay_logit.astype(jnp.float32))
    o_f, s_f = retention_scan(q, k, v, lg[0], s0_f)
    o_b, s_b = retention_scan(jnp.flip(q, 1), jnp.flip(k, 1), jnp.flip(v, 1), lg[1], s0_b)
    return o_f + jnp.flip(o_b, 1), s_f, s_b


def ret_output(o, g, gn_w, w_out, dtype):
    B, N = o.shape[:2]
    mu = jnp.mean(o, axis=-1, keepdims=True)
    var = jnp.mean(jnp.square(o - mu), axis=-1, keepdims=True)
    y = ((o - mu) * lax.rsqrt(var + EPS)).reshape(B, N, RET_WIDTH) * gn_w.astype(jnp.float32)
    y = y.astype(dtype) * jax.nn.silu(g)
    return y @ w_out


def ret_context(h, w_in, w_out, decay_logit, gn_w):
    B = h.shape[0]
    q, k, v, g = ret_project(h, w_in)
    s0 = jnp.zeros((B, RET_HEADS, RET_QK_DIM, RET_V_DIM), jnp.float32)
    o, s_f, s_b = ret_bidirectional(q, k, v, decay_logit, s0, s0)
    state = jnp.stack([s_f, s_b], axis=1).astype(h.dtype)
    return ret_output(o, g, gn_w, w_out, h.dtype), state


def ret_latent(h, state, w_in, w_out, decay_logit, gn_w):
    q, k, v, g = ret_project(h, w_in)
    q = axial_rope(q)
    k = axial_rope(k)
    o, _, _ = ret_bidirectional(q, k, v, decay_logit, state[:, 0], state[:, 1])
    return ret_output(o, g, gn_w, w_out, h.dtype)


def chunk_mlp(h, w_in, ln_w, ln_b, w_s, b_s, w_out):
    B, N, _ = h.shape
    nc = N // MLP_CHUNK
    u, v, g = jnp.split(h @ w_in, 3, axis=-1)
    u = jax.nn.gelu(u)
    v = layer_norm(jax.nn.gelu(v), ln_w, ln_b)
    vg = v.reshape(B, nc, MLP_CHUNK, MLP_GROUPS, MLP_GROUP_DIM)
    sv = jnp.einsum('gij,bnjgd->bnigd', w_s, vg) + b_s.T[None, None, :, :, None]
    o = u * sv.reshape(B, N, MLP_WIDTH) * jax.nn.silu(g)
    return o @ w_out


def modulate(h, shift, scale):
    return h * (1.0 + scale) + shift


def setup_inputs(seed: int = 0) -> dict:
    key = jax.random.key(seed)
    ks = jax.random.split(key, 26)
    f32 = jnp.float32

    def nrm(k, shape, s):
        return jax.random.normal(k, shape, f32) * s

    dec_base = jnp.log(2.0 ** (5.0 + jnp.arange(RET_HEADS, dtype=f32)) - 1.0)
    return {
        "x_prompt": nrm(ks[0], (BATCH, SEQ, D_MODEL), 1.0),
        "x_sample": nrm(ks[1], (DEC_BATCH, DEC_SEQ, D_MODEL), 1.0),
        "cache_na_k": nrm(ks[2], (DEC_BATCH, N_NA_LAYERS, PAST_LEN, NA_HEADS, NA_HEAD_DIM), 1.0),
        "cache_na_v": nrm(ks[3], (DEC_BATCH, N_NA_LAYERS, PAST_LEN, NA_HEADS, NA_HEAD_DIM), 1.0),
        "state_ret": nrm(ks[4], (DEC_BATCH, N_RET_LAYERS, 2, RET_HEADS, RET_QK_DIM, RET_V_DIM), 0.5),
        "c": nrm(ks[5], (DEC_BATCH, D_MODEL), 1.0),
        "c_ctx": nrm(ks[6], (D_MODEL,), 1.0),
        "norm_w": 1.0 + nrm(ks[7], (DEPTH, D_MODEL), 0.02),
        "w_ada": nrm(ks[8], (DEPTH, D_MODEL, 3 * D_MODEL), 0.5 * D_MODEL ** -0.5),
        "b_ada": nrm(ks[9], (DEPTH, 3 * D_MODEL), 0.01),
        "na_w_in": nrm(ks[10], (N_NA_LAYERS, D_MODEL, 4 * NA_WIDTH), D_MODEL ** -0.5),
        "na_w_out": nrm(ks[11], (N_NA_LAYERS, NA_WIDTH, D_MODEL), NA_WIDTH ** -0.5),
        "na_q_gain": 1.0 + nrm(ks[12], (N_NA_LAYERS, NA_HEAD_DIM), 0.02),
        "na_k_gain": 1.0 + nrm(ks[13], (N_NA_LAYERS, NA_HEAD_DIM), 0.02),
        "na_rpb": nrm(ks[14], (N_NA_LAYERS, NA_HEADS, 2 * NA_KH - 1, 2 * NA_KW - 1), 0.1),
        "ret_w_in": nrm(ks[15], (N_RET_LAYERS, D_MODEL, 2 * RET_QK_WIDTH + 2 * RET_WIDTH), D_MODEL ** -0.5),
        "ret_w_out": nrm(ks[16], (N_RET_LAYERS, RET_WIDTH, D_MODEL), RET_WIDTH ** -0.5),
        "ret_decay_logit": dec_base[None, None, :] + nrm(ks[17], (N_RET_LAYERS, 2, RET_HEADS), 0.1),
        "ret_gn_w": 1.0 + nrm(ks[18], (N_RET_LAYERS, RET_WIDTH), 0.02),
        "mlp_w_in": nrm(ks[19], (N_MLP_LAYERS, D_MODEL, 3 * MLP_WIDTH), D_MODEL ** -0.5),
        "mlp_ln_w": 1.0 + nrm(ks[20], (N_MLP_LAYERS, MLP_WIDTH), 0.02),
        "mlp_ln_b": nrm(ks[21], (N_MLP_LAYERS, MLP_WIDTH), 0.01),
        "mlp_w_s": nrm(ks[22], (N_MLP_LAYERS, MLP_GROUPS, MLP_CHUNK, MLP_CHUNK), MLP_CHUNK ** -0.5),
        "mlp_b_s": 1.0 + nrm(ks[23], (N_MLP_LAYERS, MLP_GROUPS, MLP_CHUNK), 0.02),
        "mlp_w_out": nrm(ks[24], (N_MLP_LAYERS, MLP_WIDTH, D_MODEL), MLP_WIDTH ** -0.5),
    }


def reference(x_prompt, x_sample, cache_na_k, cache_na_v, state_ret, c, c_ctx, norm_w, w_ada, b_ada,
              na_w_in, na_w_out, na_q_gain, na_k_gain, na_rpb,
              ret_w_in, ret_w_out, ret_decay_logit, ret_gn_w,
              mlp_w_in, mlp_ln_w, mlp_ln_b, mlp_w_s, mlp_b_s, mlp_w_out):
    yp = x_prompt
    ys = x_sample
    new_k, new_v, new_s = [], [], []
    for i in range(DEPTH):
        kind = i % N_MIXERS
        j = i // N_MIXERS
        sh_p, sc_p, g_p = adaln(c_ctx, w_ada[i], b_ada[i])
        sh_s, sc_s, g_s = adaln(c, w_ada[i], b_ada[i])
        sh_s, sc_s, g_s = sh_s[:, None, :], sc_s[:, None, :], g_s[:, None, :]
        hp = modulate(rms_norm(yp, norm_w[i]), sh_p, sc_p)
        hs = modulate(rms_norm(ys, norm_w[i]), sh_s, sc_s)
        if kind == 0:
            op, kc, vc = na_context(hp, na_w_in[j], na_w_out[j], na_q_gain[j], na_k_gain[j])
            os_ = na_latent(hs, cache_na_k[:, j], cache_na_v[:, j], na_w_in[j], na_w_out[j],
                            na_q_gain[j], na_k_gain[j], na_rpb[j])
            new_k.append(kc)
            new_v.append(vc)
        elif kind == 1:
            op, st = ret_context(hp, ret_w_in[j], ret_w_out[j], ret_decay_logit[j], ret_gn_w[j])
            os_ = ret_latent(hs, state_ret[:, j], ret_w_in[j], ret_w_out[j], ret_decay_logit[j], ret_gn_w[j])
            new_s.append(st)
        else:
            op = chunk_mlp(hp, mlp_w_in[j], mlp_ln_w[j], mlp_ln_b[j], mlp_w_s[j], mlp_b_s[j], mlp_w_out[j])
            os_ = chunk_mlp(hs, mlp_w_in[j], mlp_ln_w[j], mlp_ln_b[j], mlp_w_s[j], mlp_b_s[j], mlp_w_out[j])
        yp = yp + g_p * op
        ys = ys + g_s * os_
    new_na_k = jnp.stack(new_k, axis=1)
    new_na_v = jnp.stack(new_v, axis=1)
    new_ret = jnp.stack(new_s, axis=1)
    return (yp, ys, new_na_k, new_na_v, new_ret)
```

```python
import functools
import math

import numpy as np
import jax
import jax.numpy as jnp
from jax import lax
from jax.experimental import pallas as pl
from jax.experimental.pallas import tpu as pltpu

F32 = jnp.float32
BF16 = jnp.bfloat16

D_MODEL = 1024
DEPTH = 4
N_MIXERS = 3
GRID_W = 64
EPS = 1e-6
NEG_INF = -1e30

NA_HEADS = 16
NA_HEAD_DIM = 64
NA_KH = 8
NA_KW = 16
NA_Q_ROWS = 4
NA_WIN_ROWS = 12
NA_TQ = NA_Q_ROWS * GRID_W
NA_WIN = NA_WIN_ROWS * GRID_W

RET_HEADS = 4
RET_QK_DIM = 256
RET_V_DIM = 512
RET_QK_WIDTH = RET_HEADS * RET_QK_DIM
RET_WIDTH = RET_HEADS * RET_V_DIM
ROPE_BASE = 10000.0

MLP_WIDTH = 2048
MLP_GROUPS = 8
MLP_GROUP_DIM = 256
MLP_CHUNK = 128

TOKEN_BLOCK = 256
V7X_VMEM_LIMIT = 56 * 1024 * 1024

_NT = (((1,), (1,)), ((), ()))
_TN = (((0,), (0,)), ((), ()))


def _dot(a, b):
    return jnp.dot(a, b, preferred_element_type=F32)


def _dot_nt(a, b):
    return lax.dot_general(a, b, _NT, preferred_element_type=F32)


def _silu(x):
    return x * jax.nn.sigmoid(x)


def _gelu_tanh(x):
    c = math.sqrt(2.0 / math.pi)
    return x * (0.5 * (1.0 + jnp.tanh(c * (x + 0.044715 * (x * x * x)))))


def _norm_mod(x, nw, mod):
    ms = jnp.mean(x * x, axis=-1, keepdims=True)
    y = x * lax.rsqrt(ms + EPS) * nw
    return y * (1.0 + mod[1:2]) + mod[0:1]


def _const_spec(shape):
    nd = len(shape)
    return pl.BlockSpec(shape, lambda *_: (0,) * nd, pipeline_mode=pl.Buffered(1))


def _params(n_axes, vmem=V7X_VMEM_LIMIT):
    return pltpu.CompilerParams(dimension_semantics=("arbitrary",) * n_axes,
                                vmem_limit_bytes=vmem)


def _ada_kernel(c_ref, w_ref, b_ref, o_ref):
    a = _silu(c_ref[...]).astype(BF16)
    o_ref[0] = _dot(a, w_ref[0].astype(BF16)) + b_ref[0]


def _ada_all(c_rows, w_ada, b_ada):
    n_col = 3
    return pl.pallas_call(
        _ada_kernel,
        grid=(DEPTH, n_col),
        in_specs=[
            pl.BlockSpec((8, D_MODEL), lambda i, n: (0, 0)),
            pl.BlockSpec((1, D_MODEL, D_MODEL), lambda i, n: (i, 0, n)),
            pl.BlockSpec((1, 1, D_MODEL), lambda i, n: (i, 0, n)),
        ],
        out_specs=pl.BlockSpec((1, 8, D_MODEL), lambda i, n: (i, 0, n)),
        out_shape=jax.ShapeDtypeStruct((DEPTH, 8, 3 * D_MODEL), F32),
        compiler_params=_params(2, 32 * 1024 * 1024),
        name="adaln",
    )(c_rows, w_ada, b_ada.reshape(DEPTH, 1, 3 * D_MODEL))


def _na_proj_kernel(x_ref, mod_ref, nw_ref, w_ref, qg_ref, kg_ref, bd_ref, *outs, emit_f32):
    q_out, k_out, v_out, g_out = outs[:4]
    h = _norm_mod(x_ref[...], nw_ref[...], mod_ref[0]).astype(BF16)

    def head_norm(t, gain):
        ssq = _dot((t * t).astype(BF16), bd_ref[...])
        return t * lax.rsqrt(ssq * (1.0 / NA_HEAD_DIM) + EPS) * gain

    w = D_MODEL
    qn = head_norm(_dot(h, w_ref[:, 0:w]), qg_ref[...])
    q_out[...] = (qn * (NA_HEAD_DIM ** -0.5)).astype(BF16)
    kn = head_norm(_dot(h, w_ref[:, w:2 * w]), kg_ref[...])
    k_out[...] = kn.astype(BF16)
    v = _dot(h, w_ref[:, 2 * w:3 * w])
    v_out[...] = v.astype(BF16)
    g_out[...] = _silu(_dot(h, w_ref[:, 3 * w:4 * w])).astype(BF16)
    if emit_f32:
        outs[4][...] = kn
        outs[5][...] = v


def _na_proj(x, mod, nw, w_in, q_gain, k_gain, bd, blocks_per_batch, emit_f32):
    t = x.shape[0]
    tm = TOKEN_BLOCK
    tok = pl.BlockSpec((tm, D_MODEL), lambda i: (i, 0))
    n_out = 6 if emit_f32 else 4
    out_shape = [jax.ShapeDtypeStruct((t, D_MODEL), BF16)] * 4
    if emit_f32:
        out_shape += [jax.ShapeDtypeStruct((t, D_MODEL), F32)] * 2
    return pl.pallas_call(
        functools.partial(_na_proj_kernel, emit_f32=emit_f32),
        grid=(t // tm,),
        in_specs=[
            tok,
            pl.BlockSpec((1, 3, D_MODEL), lambda i: (i // blocks_per_batch, 0, 0)),
            _const_spec((1, D_MODEL)),
            _const_spec((D_MODEL, 4 * D_MODEL)),
            _const_spec((1, D_MODEL)),
            _const_spec((1, D_MODEL)),
            _const_spec((D_MODEL, D_MODEL)),
        ],
        out_specs=[tok] * n_out,
        out_shape=out_shape,
        compiler_params=_params(1),
        name="na_proj",
    )(x, mod, nw, w_in, q_gain, k_gain, bd)


def _pair_masks():
    lane = lax.broadcasted_iota(jnp.int32, (1, 2 * NA_HEAD_DIM), 1)
    return lane < NA_HEAD_DIM


def _na_prompt_attn_kernel(q_ref, k_ref, v_ref, g_ref, x_ref, mod_ref, wo_ref, y_ref, og_ref):
    first = _pair_masks()
    pw = 2 * NA_HEAD_DIM
    for p in range(NA_HEADS // 2):
        sl = slice(p * pw, (p + 1) * pw)
        q2, k2, v2 = q_ref[:, sl], k_ref[:, sl], v_ref[:, sl]
        outs = []
        for qh in (jnp.where(first, q2, jnp.zeros_like(q2)), jnp.where(first, jnp.zeros_like(q2), q2)):
            s = _dot_nt(qh, k2)
            e = jnp.exp(s - jnp.max(s, axis=-1, keepdims=True))
            l = jnp.sum(e, axis=-1, keepdims=True)
            outs.append(_dot(e.astype(BF16), v2) / l)
        o2 = jnp.where(first, outs[0], outs[1])
        og_ref[:, sl] = (o2 * g_ref[:, sl].astype(F32)).astype(BF16)
    y_ref[...] = x_ref[...] + mod_ref[0][2:3] * _dot(og_ref[...], wo_ref[...])


def _na_prompt_attn(q, k, v, g, x, mod, w_out):
    t = x.shape[0]
    tm = TOKEN_BLOCK
    tok = pl.BlockSpec((tm, D_MODEL), lambda i: (i, 0))
    return pl.pallas_call(
        _na_prompt_attn_kernel,
        grid=(t // tm,),
        in_specs=[tok, tok, tok, tok, tok,
                  _const_spec((1, 3, D_MODEL)),
                  _const_spec((D_MODEL, D_MODEL))],
        out_specs=tok,
        out_shape=jax.ShapeDtypeStruct((t, D_MODEL), F32),
        scratch_shapes=[pltpu.VMEM((tm, D_MODEL), BF16)],
        compiler_params=_params(1),
        name="na_prompt_attn",
    )(q, k, v, g, x, mod, w_out)


def _na_latent_attn_kernel(q_ref, k_ref, v_ref, kc_ref, vc_ref, bias_ref, g_ref, x_ref, mod_ref,
                           wo_ref, y_ref, og_ref):
    i = pl.program_id(0)
    win0 = pl.multiple_of((i // 2) * NA_TQ, NA_TQ)
    first = _pair_masks()
    pw = 2 * NA_HEAD_DIM
    for p in range(NA_HEADS // 2):
        sl = slice(p * pw, (p + 1) * pw)
        q2 = q_ref[:, sl]
        kw = k_ref[0, pl.ds(win0, NA_WIN), sl]
        vw = v_ref[0, pl.ds(win0, NA_WIN), sl]
        kc, vc = kc_ref[0, :, sl], vc_ref[0, :, sl]
        outs = []
        for hh, qh in enumerate((jnp.where(first, q2, jnp.zeros_like(q2)),
                                 jnp.where(first, jnp.zeros_like(q2), q2))):
            s_loc = _dot_nt(qh, kw) + bias_ref[2 * p + hh, 0].astype(F32)
            s_ctx = _dot_nt(qh, kc)
            m = jnp.maximum(jnp.max(s_loc, axis=-1, keepdims=True),
                            jnp.max(s_ctx, axis=-1, keepdims=True))
            e_loc = jnp.exp(s_loc - m)
            e_ctx = jnp.exp(s_ctx - m)
            l = jnp.sum(e_loc, axis=-1, keepdims=True) + jnp.sum(e_ctx, axis=-1, keepdims=True)
            outs.append((_dot(e_loc.astype(BF16), vw) + _dot(e_ctx.astype(BF16), vc)) / l)
        o2 = jnp.where(first, outs[0], outs[1])
        og_ref[:, sl] = (o2 * g_ref[:, sl].astype(F32)).astype(BF16)
    y_ref[...] = x_ref[...] + mod_ref[0][2:3] * _dot(og_ref[...], wo_ref[...])


def _na_latent_attn(q, k, v, kc, vc, bias, g, x, mod, w_out):
    nb, n = k.shape[0], k.shape[1]
    nq = n // NA_TQ
    tok = pl.BlockSpec((NA_TQ, D_MODEL), lambda i, b: (b * nq + i, 0))
    full = pl.BlockSpec((1, n, D_MODEL), lambda i, b: (b, 0, 0))
    ctx = pl.BlockSpec((1, kc.shape[1], D_MODEL), lambda i, b: (b, 0, 0))
    return pl.pallas_call(
        _na_latent_attn_kernel,
        grid=(nq, nb),
        in_specs=[tok, full, full, ctx, ctx,
                  pl.BlockSpec((NA_HEADS, 1, NA_TQ, NA_WIN), lambda i, b: (0, i, 0, 0)),
                  tok, tok,
                  pl.BlockSpec((1, 3, D_MODEL), lambda i, b: (b, 0, 0)),
                  _const_spec((D_MODEL, D_MODEL))],
        out_specs=tok,
        out_shape=jax.ShapeDtypeStruct((nb * n, D_MODEL), F32),
        scratch_shapes=[pltpu.VMEM((NA_TQ, D_MODEL), BF16)],
        compiler_params=_params(2),
        name="na_latent_attn",
    )(q, k, v, kc, vc, bias, g, x, mod, w_out)


def _na_bias_tables(rpb, rows):
    n_blk = rows // NA_Q_ROWS
    kh = min(NA_KH, rows)
    cq = np.arange(GRID_W)
    cstart = np.clip(cq - NA_KW // 2, 0, GRID_W - NA_KW)
    col_ok = (cq[None, :] >= cstart[:, None]) & (cq[None, :] < cstart[:, None] + NA_KW)
    coff = np.clip(cq[None, :] - cq[:, None], -(NA_KW - 1), NA_KW - 1) + (NA_KW - 1)
    t = jnp.where(col_ok[None, None], rpb[:, :, coff], NEG_INF)
    qr = (np.arange(n_blk)[:, None] * NA_Q_ROWS + np.arange(NA_Q_ROWS)[None, :])
    win_row0 = (np.arange(n_blk) // 2) * NA_Q_ROWS
    kr = win_row0[:, None] + np.arange(NA_WIN_ROWS)[None, :]
    rstart = np.clip(qr - kh // 2, 0, rows - kh)
    row_ok = (kr[:, None, :] >= rstart[:, :, None]) & (kr[:, None, :] < rstart[:, :, None] + kh)
    roff = np.clip(kr[:, None, :] - qr[:, :, None] + (NA_KH - 1), 0, 2 * NA_KH - 2)
    b = t[:, roff]
    b = jnp.where(row_ok[None, :, :, :, None, None], b, NEG_INF)
    b = b.transpose(0, 1, 2, 4, 3, 5).reshape(NA_HEADS, n_blk, NA_TQ, NA_WIN)
    return b.astype(BF16)


def _ret_proj_kernel(x_ref, mod_ref, nw_ref, w_ref, *rest, rope):
    if rope:
        cos_ref, sin_ref, q_out, k_out, v_out, g_out = rest
    else:
        q_out, k_out, v_out, g_out = rest
    h = _norm_mod(x_ref[...], nw_ref[...], mod_ref[0]).astype(BF16)
    qw, vw = RET_QK_WIDTH, RET_WIDTH

    def rot(t):
        if not rope:
            return t
        cos, sin = cos_ref[...], sin_ref[...]
        parts = []
        for c in range(qw // 128):
            sl = slice(c * 128, (c + 1) * 128)
            tc = t[:, sl]
            parts.append(tc * cos[:, sl] + pltpu.roll(tc, 64, axis=1) * sin[:, sl])
        return jnp.concatenate(parts, axis=1)

    q_out[...] = rot(_dot(h, w_ref[:, 0:qw])).astype(BF16)
    k_out[...] = (rot(_dot(h, w_ref[:, qw:2 * qw])) * (RET_QK_DIM ** -0.5)).astype(BF16)
    v_out[...] = _dot(h, w_ref[:, 2 * qw:2 * qw + vw]).astype(BF16)
    g_out[...] = _silu(_dot(h, w_ref[:, 2 * qw + vw:2 * qw + 2 * vw])).astype(BF16)


def _ret_proj(x, mod, nw, w_in, blocks_per_batch, rope_tables):
    t = x.shape[0]
    tm = TOKEN_BLOCK
    tok = pl.BlockSpec((tm, D_MODEL), lambda i: (i, 0))
    tok2 = pl.BlockSpec((tm, RET_WIDTH), lambda i: (i, 0))
    rope = rope_tables is not None
    in_specs = [tok,
                pl.BlockSpec((1, 3, D_MODEL), lambda i: (i // blocks_per_batch, 0, 0)),
                _const_spec((1, D_MODEL)),
                _const_spec((D_MODEL, 2 * RET_QK_WIDTH + 2 * RET_WIDTH))]
    args = [x, mod, nw, w_in]
    if rope:
        pos = pl.BlockSpec((tm, RET_QK_WIDTH), lambda i: (i % blocks_per_batch, 0))
        in_specs += [pos, pos]
        args += list(rope_tables)
    return pl.pallas_call(
        functools.partial(_ret_proj_kernel, rope=rope),
        grid=(t // tm,),
        in_specs=in_specs,
        out_specs=[tok, tok, tok2, tok2],
        out_shape=[jax.ShapeDtypeStruct((t, RET_QK_WIDTH), BF16)] * 2
        + [jax.ShapeDtypeStruct((t, RET_WIDTH), BF16)] * 2,
        compiler_params=_params(1),
        name="ret_proj",
    )(*args)


def _rope_tables(n):
    half = RET_QK_DIM // 2
    t = np.arange(n)
    inv = ROPE_BASE ** (-jnp.arange(0, half, 2, dtype=F32) / half)
    tabs = []
    for pos in ((t // GRID_W).astype(np.float32), (t % GRID_W).astype(np.float32)):
        ang = jnp.asarray(pos)[:, None] * inv[None, :]
        tabs.append((jnp.cos(ang), jnp.sin(ang)))
    cos = jnp.concatenate([tabs[0][0], tabs[0][0], tabs[1][0], tabs[1][0]], axis=1)
    sin = jnp.concatenate([-tabs[0][1], tabs[0][1], -tabs[1][1], tabs[1][1]], axis=1)
    return jnp.tile(cos, (1, RET_HEADS)), jnp.tile(sin, (1, RET_HEADS))


def _ret_core_kernel(lg_ref, q_ref, k_ref, v_ref, g_ref, x_ref, mod_ref, gn_ref, wo_ref, *rest,
                     n, has_s0, emit_state):
    rest = list(rest)
    s0_ref = rest.pop(0) if has_s0 else None
    y_ref = rest.pop(0)
    st_ref = rest.pop(0) if emit_state else None
    yg_ref = rest.pop(0)
    tq = q_ref.shape[0]
    row0 = pl.program_id(1) * tq
    qi = (row0 + lax.broadcasted_iota(jnp.int32, (tq, n), 0)).astype(F32)
    kj = lax.broadcasted_iota(jnp.int32, (tq, n), 1).astype(F32)
    diff = qi - kj
    for h in range(RET_HEADS):
        lgf, lgb = lg_ref[h], lg_ref[RET_HEADS + h]
        qh = q_ref[:, h * RET_QK_DIM:(h + 1) * RET_QK_DIM]
        kh = k_ref[0, :, h * RET_QK_DIM:(h + 1) * RET_QK_DIM]
        vh = v_ref[0, :, h * RET_V_DIM:(h + 1) * RET_V_DIM]
        dec = jnp.exp(jnp.where(diff >= 0.0, lgf * diff, -lgb * diff))
        dec = jnp.where(diff == 0.0, 2.0, dec)
        inner = (_dot_nt(qh, kh) * dec).astype(BF16)
        o = _dot(inner, vh)
        if has_s0:
            pos = qi[:, 0:1]
            o = o + _dot(qh, s0_ref[0, 0, h]) * jnp.exp(lgf * (pos + 1.0))
            o = o + _dot(qh, s0_ref[0, 1, h]) * jnp.exp(lgb * (float(n) - pos))
        if emit_state:
            kpos = lax.broadcasted_iota(jnp.int32, (n, RET_QK_DIM), 0).astype(F32)
            kf = kh.astype(F32)
            kdf = (kf * jnp.exp(lgf * (float(n - 1) - kpos))).T.astype(BF16)
            kdb = (kf * jnp.exp(lgb * kpos)).T.astype(BF16)
            st_ref[0, 0, h] = _dot(kdf, vh)
            st_ref[0, 1, h] = _dot(kdb, vh)
        mu = jnp.mean(o, axis=-1, keepdims=True)
        oc = o - mu
        var = jnp.mean(oc * oc, axis=-1, keepdims=True)
        vs = slice(h * RET_V_DIM, (h + 1) * RET_V_DIM)
        yn = oc * lax.rsqrt(var + EPS) * gn_ref[:, vs]
        yg_ref[:, vs] = (yn * g_ref[:, vs].astype(F32)).astype(BF16)
    y_ref[...] = x_ref[...] + mod_ref[0][2:3] * _dot(yg_ref[...], wo_ref[...])


def _ret_core(lg, q, k, v, g, x, mod, gn_w, w_out, n, mod_per_batch, s0=None, emit_state=False):
    t = x.shape[0]
    nb = t // n
    tq = TOKEN_BLOCK
    nq = n // tq
    tok = pl.BlockSpec((tq, D_MODEL), lambda b, i: (b * nq + i, 0))
    tok2 = pl.BlockSpec((tq, RET_WIDTH), lambda b, i: (b * nq + i, 0))
    in_specs = [pl.BlockSpec(memory_space=pltpu.SMEM),
                tok,
                pl.BlockSpec((1, n, RET_QK_WIDTH), lambda b, i: (b, 0, 0)),
                pl.BlockSpec((1, n, RET_WIDTH), lambda b, i: (b, 0, 0)),
                tok2, tok,
                pl.BlockSpec((1, 3, D_MODEL), (lambda b, i: (b, 0, 0)) if mod_per_batch else (lambda b, i: (0, 0, 0))),
                _const_spec((1, RET_WIDTH)),
                _const_spec((RET_WIDTH, D_MODEL))]
    args = [lg, q, k.reshape(nb, n, RET_QK_WIDTH), v.reshape(nb, n, RET_WIDTH), g, x, mod, gn_w, w_out]
    st_block = (1, 2, RET_HEADS, RET_QK_DIM, RET_V_DIM)
    if s0 is not None:
        in_specs.append(pl.BlockSpec(st_block, lambda b, i: (b, 0, 0, 0, 0)))
        args.append(s0)
    out_specs = [tok]
    out_shape = [jax.ShapeDtypeStruct((t, D_MODEL), F32)]
    if emit_state:
        assert nq == 1
        out_specs.append(pl.BlockSpec(st_block, lambda b, i: (b, 0, 0, 0, 0)))
        out_shape.append(jax.ShapeDtypeStruct((nb,) + st_block[1:], F32))
    return pl.pallas_call(
        functools.partial(_ret_core_kernel, n=n, has_s0=s0 is not None, emit_state=emit_state),
        grid=(nb, nq),
        in_specs=in_specs,
        out_specs=out_specs,
        out_shape=out_shape,
        scratch_shapes=[pltpu.VMEM((tq, RET_WIDTH), BF16)],
        compiler_params=_params(2),
        name="ret_core",
    )(*args)


def _mlp_kernel(x_ref, mod_ref, nw_ref, wi_ref, lnw_ref, lnb_ref, ws_ref, bs_ref, wo_ref, y_ref, o_ref):
    x = x_ref[...]
    mod = mod_ref[0]
    h = _norm_mod(x, nw_ref[...], mod).astype(BF16)
    w = MLP_WIDTH
    v = _gelu_tanh(_dot(h, wi_ref[:, w:2 * w]))
    mu = jnp.mean(v, axis=-1, keepdims=True)
    vc = v - mu
    var = jnp.mean(vc * vc, axis=-1, keepdims=True)
    vn = (vc * lax.rsqrt(var + EPS) * lnw_ref[...] + lnb_ref[...]).astype(BF16)
    u = _gelu_tanh(_dot(h, wi_ref[:, 0:w]))
    ug = u * _silu(_dot(h, wi_ref[:, 2 * w:3 * w]))
    tm = x.shape[0]
    for c in range(tm // MLP_CHUNK):
        rs = slice(c * MLP_CHUNK, (c + 1) * MLP_CHUNK)
        for g in range(MLP_GROUPS):
            cs = slice(g * MLP_GROUP_DIM, (g + 1) * MLP_GROUP_DIM)
            sv = _dot(ws_ref[g], vn[rs, cs]) + bs_ref[:, cs]
            o_ref[rs, cs] = (ug[rs, cs] * sv).astype(BF16)
    y_ref[...] = x + mod[2:3] * _dot(o_ref[...], wo_ref[...])


def _mlp_layer(x, mod, nw, w_in, ln_w, ln_b, w_s, b_s_cols, w_out, blocks_per_batch):
    t = x.shape[0]
    tm = TOKEN_BLOCK
    tok = pl.BlockSpec((tm, D_MODEL), lambda i: (i, 0))
    return pl.pallas_call(
        _mlp_kernel,
        grid=(t // tm,),
        in_specs=[tok,
                  pl.BlockSpec((1, 3, D_MODEL), lambda i: (i // blocks_per_batch, 0, 0)),
                  _const_spec((1, D_MODEL)),
                  _const_spec((D_MODEL, 3 * MLP_WIDTH)),
                  _const_spec((1, MLP_WIDTH)),
                  _const_spec((1, MLP_WIDTH)),
                  _const_spec((MLP_GROUPS, MLP_CHUNK, MLP_CHUNK)),
                  _const_spec((MLP_CHUNK, MLP_WIDTH)),
                  _const_spec((MLP_WIDTH, D_MODEL))],
        out_specs=tok,
        out_shape=jax.ShapeDtypeStruct((t, D_MODEL), F32),
        scratch_shapes=[pltpu.VMEM((tm, MLP_WIDTH), BF16)],
        compiler_params=_params(1),
        name="mlp_layer",
    )(x, mod, nw, w_in, ln_w, ln_b, w_s, b_s_cols, w_out)


def kernel(x_prompt, x_sample, cache_na_k, cache_na_v, state_ret, c, c_ctx, norm_w, w_ada, b_ada,
           na_w_in, na_w_out, na_q_gain, na_k_gain, na_rpb,
           ret_w_in, ret_w_out, ret_decay_logit, ret_gn_w,
           mlp_w_in, mlp_ln_w, mlp_ln_b, mlp_w_s, mlp_b_s, mlp_w_out):
    batch, seq, _ = x_prompt.shape
    dec_batch, dec_seq, _ = x_sample.shape
    past = cache_na_k.shape[2]
    assert seq == TOKEN_BLOCK and dec_seq == 16 * GRID_W and dec_batch <= 7

    c_rows = jnp.zeros((8, D_MODEL), F32).at[:dec_batch].set(c).at[dec_batch].set(c_ctx)
    mods = _ada_all(c_rows, w_ada, b_ada).reshape(DEPTH, 8, 3, D_MODEL)

    lane_head = np.arange(D_MODEL) // NA_HEAD_DIM
    bd = jnp.asarray(lane_head[:, None] == lane_head[None, :], dtype=BF16)
    rope_tabs = _rope_tables(dec_seq)

    yp = x_prompt.reshape(batch * seq, D_MODEL)
    ys = x_sample.reshape(dec_batch * dec_seq, D_MODEL)
    bpb_s = dec_seq // TOKEN_BLOCK
    new_k, new_v, new_s = [], [], []
    for i in range(DEPTH):
        kind, j = i % N_MIXERS, i // N_MIXERS
        mod_p = mods[i, dec_batch:dec_batch + 1]
        mod_s = mods[i, :dec_batch]
        nw = norm_w[i].reshape(1, D_MODEL)
        if kind == 0:
            w_in = na_w_in[j].astype(BF16)
            w_out = na_w_out[j].astype(BF16)
            qg = jnp.tile(na_q_gain[j], NA_HEADS).reshape(1, D_MODEL)
            kg = jnp.tile(na_k_gain[j], NA_HEADS).reshape(1, D_MODEL)
            q, k, v, g, kf, vf = _na_proj(yp, mod_p, nw, w_in, qg, kg, bd, batch, True)
            new_k.append(kf.reshape(batch, seq, NA_HEADS, NA_HEAD_DIM))
            new_v.append(vf.reshape(batch, seq, NA_HEADS, NA_HEAD_DIM))
            yp = _na_prompt_attn(q, k, v, g, yp, mod_p, w_out)
            q, k, v, g = _na_proj(ys, mod_s, nw, w_in, qg, kg, bd, bpb_s, False)
            kc = cache_na_k[:, j].reshape(dec_batch, past, D_MODEL).astype(BF16)
            vc = cache_na_v[:, j].reshape(dec_batch, past, D_MODEL).astype(BF16)
            ys = _na_latent_attn(q, k.reshape(dec_batch, dec_seq, D_MODEL),
                                 v.reshape(dec_batch, dec_seq, D_MODEL), kc, vc,
                                 _na_bias_tables(na_rpb[j], dec_seq // GRID_W), g, ys, mod_s, w_out)
        elif kind == 1:
            w_in = ret_w_in[j].astype(BF16)
            w_out = ret_w_out[j].astype(BF16)
            lg = jax.nn.log_sigmoid(ret_decay_logit[j].astype(F32)).reshape(2 * RET_HEADS)
            gn = ret_gn_w[j].reshape(1, RET_WIDTH)
            q, k, v, g = _ret_proj(yp, mod_p, nw, w_in, batch, None)
            yp, st = _ret_core(lg, q, k, v, g, yp, mod_p, gn, w_out, seq, False, emit_state=True)
            new_s.append(st)
            q, k, v, g = _ret_proj(ys, mod_s, nw, w_in, bpb_s, rope_tabs)
            (ys,) = _ret_core(lg, q, k, v, g, ys, mod_s, gn, w_out, dec_seq, True,
                              s0=state_ret[:, j].astype(BF16))
        else:
            w_in = mlp_w_in[j].astype(BF16)
            w_out = mlp_w_out[j].astype(BF16)
            lnw = mlp_ln_w[j].reshape(1, MLP_WIDTH)
            lnb = mlp_ln_b[j].reshape(1, MLP_WIDTH)
            w_s = mlp_w_s[j].astype(BF16)
            b_cols = jnp.repeat(mlp_b_s[j].T, MLP_GROUP_DIM, axis=1)
            yp = _mlp_layer(yp, mod_p, nw, w_in, lnw, lnb, w_s, b_cols, w_out, batch)
            ys = _mlp_layer(ys, mod_s, nw, w_in, lnw, lnb, w_s, b_cols, w_out, bpb_s)
    return (yp.reshape(batch, seq, D_MODEL),
            ys.reshape(dec_batch, dec_seq, D_MODEL),
            jnp.stack(new_k, axis=1),
            jnp.stack(new_v, axis=1),
            jnp.stack(new_s, axis=1))
```

```python
import functools
import math

import numpy as np
import jax
import jax.numpy as jnp
from jax import lax
from jax.experimental import pallas as pl
from jax.experimental.pallas import tpu as pltpu

F32 = jnp.float32
BF16 = jnp.bfloat16

D_MODEL = 1024
DEPTH = 4
N_MIXERS = 3
GRID_W = 64
EPS = 1e-6
NEG_INF = -1e30

NA_HEADS = 16
NA_HEAD_DIM = 64
NA_KH = 8
NA_KW = 16
NA_Q_ROWS = 4
NA_WIN_ROWS = 12
NA_TQ = NA_Q_ROWS * GRID_W
NA_WIN = NA_WIN_ROWS * GRID_W

RET_HEADS = 4
RET_QK_DIM = 256
RET_V_DIM = 512
RET_QK_WIDTH = RET_HEADS * RET_QK_DIM
RET_WIDTH = RET_HEADS * RET_V_DIM
ROPE_BASE = 10000.0

MLP_WIDTH = 2048
MLP_GROUPS = 8
MLP_GROUP_DIM = 256
MLP_CHUNK = 128

TOKEN_BLOCK = 256
V7X_VMEM_LIMIT = 56 * 1024 * 1024

_NT = (((1,), (1,)), ((), ()))
_TN = (((0,), (0,)), ((), ()))


def _dot(a, b):
    return jnp.dot(a, b, preferred_element_type=F32)


def _dot_nt(a, b):
    return lax.dot_general(a, b, _NT, preferred_element_type=F32)


def _silu(x):
    return x * jax.nn.sigmoid(x)


def _gelu_tanh(x):
    c = math.sqrt(2.0 / math.pi)
    return x * (0.5 * (1.0 + jnp.tanh(c * (x + 0.044715 * (x * x * x)))))


def _norm_mod(x, nw, mod):
    ms = jnp.mean(x * x, axis=-1, keepdims=True)
    y = x * lax.rsqrt(ms + EPS) * nw
    return y * (1.0 + mod[1:2]) + mod[0:1]


def _const_spec(shape):
    nd = len(shape)
    return pl.BlockSpec(shape, lambda *_: (0,) * nd, pipeline_mode=pl.Buffered(1))


def _params(n_axes, vmem=V7X_VMEM_LIMIT):
    return pltpu.CompilerParams(dimension_semantics=("arbitrary",) * n_axes,
                                vmem_limit_bytes=vmem)


def _ada_kernel(c_ref, w_ref, b_ref, o_ref):
    a = _silu(c_ref[...]).astype(BF16)
    o_ref[0] = _dot(a, w_ref[0].astype(BF16)) + b_ref[0]


def _ada_all(c_rows, w_ada, b_ada):
    n_col = 3
    return pl.pallas_call(
        _ada_kernel,
        grid=(DEPTH, n_col),
        in_specs=[
            pl.BlockSpec((8, D_MODEL), lambda i, n: (0, 0)),
            pl.BlockSpec((1, D_MODEL, D_MODEL), lambda i, n: (i, 0, n)),
            pl.BlockSpec((1, 1, D_MODEL), lambda i, n: (i, 0, n)),
        ],
        out_specs=pl.BlockSpec((1, 8, D_MODEL), lambda i, n: (i, 0, n)),
        out_shape=jax.ShapeDtypeStruct((DEPTH, 8, 3 * D_MODEL), F32),
        compiler_params=_params(2, 32 * 1024 * 1024),
        name="adaln",
    )(c_rows, w_ada, b_ada.reshape(DEPTH, 1, 3 * D_MODEL))


def _na_proj_kernel(x_ref, mod_ref, nw_ref, w_ref, qg_ref, kg_ref, bd_ref, *rest, emit_f32, n_alias):
    outs = rest[n_alias:]
    q_out, k_out, v_out, g_out = outs[:4]
    h = _norm_mod(x_ref[...], nw_ref[...], mod_ref[0]).astype(BF16)

    def head_norm(t, gain):
        ssq = _dot((t * t).astype(BF16), bd_ref[...])
        return t * lax.rsqrt(ssq * (1.0 / NA_HEAD_DIM) + EPS) * gain

    w = D_MODEL
    qn = head_norm(_dot(h, w_ref[:, 0:w]), qg_ref[...])
    q_out[...] = (qn * (NA_HEAD_DIM ** -0.5)).astype(BF16)
    kn = head_norm(_dot(h, w_ref[:, w:2 * w]), kg_ref[...])
    k_out[...] = kn.astype(BF16)
    v = _dot(h, w_ref[:, 2 * w:3 * w])
    v_out[...] = v.astype(BF16)
    g_out[...] = _silu(_dot(h, w_ref[:, 3 * w:4 * w])).astype(BF16)
    if emit_f32:
        outs[4][0, 0] = kn.T
        outs[5][0, 0] = v.T


def _na_proj(x, mod, nw, w_in, q_gain, k_gain, bd, blocks_per_batch, cache_slot=None, cache_prev=None):
    t = x.shape[0]
    tm = TOKEN_BLOCK
    tok = pl.BlockSpec((tm, D_MODEL), lambda i: (i, 0))
    in_specs = [
        tok,
        pl.BlockSpec((1, 3, D_MODEL), lambda i: (i // blocks_per_batch, 0, 0)),
        _const_spec((1, D_MODEL)),
        _const_spec((D_MODEL, 4 * D_MODEL)),
        _const_spec((1, D_MODEL)),
        _const_spec((1, D_MODEL)),
        _const_spec((D_MODEL, D_MODEL)),
    ]
    args = [x, mod, nw, w_in, q_gain, k_gain, bd]
    out_specs = [tok] * 4
    out_shape = [jax.ShapeDtypeStruct((t, D_MODEL), BF16)] * 4
    aliases = {}
    if cache_slot is not None:
        n_layers, j = cache_slot
        out_specs += [pl.BlockSpec((1, 1, D_MODEL, tm), lambda i: (i, j, 0, 0))] * 2
        out_shape += [jax.ShapeDtypeStruct((t // tm, n_layers, D_MODEL, tm), F32)] * 2
        if cache_prev is not None:
            aliases = {len(args): 4, len(args) + 1: 5}
            in_specs += [pl.BlockSpec(memory_space=pl.ANY)] * 2
            args += list(cache_prev)
    return pl.pallas_call(
        functools.partial(_na_proj_kernel, emit_f32=cache_slot is not None, n_alias=len(aliases)),
        grid=(t // tm,),
        in_specs=in_specs,
        out_specs=out_specs,
        out_shape=out_shape,
        input_output_aliases=aliases,
        compiler_params=_params(1),
        name="na_proj",
    )(*args)


def _pair_masks():
    lane = lax.broadcasted_iota(jnp.int32, (1, 2 * NA_HEAD_DIM), 1)
    return lane < NA_HEAD_DIM


def _na_prompt_attn_kernel(q_ref, k_ref, v_ref, g_ref, x_ref, mod_ref, wo_ref, y_ref, og_ref):
    first = _pair_masks()
    pw = 2 * NA_HEAD_DIM
    for p in range(NA_HEADS // 2):
        sl = slice(p * pw, (p + 1) * pw)
        q2, k2, v2 = q_ref[:, sl], k_ref[:, sl], v_ref[:, sl]
        outs = []
        for qh in (jnp.where(first, q2, jnp.zeros_like(q2)), jnp.where(first, jnp.zeros_like(q2), q2)):
            s = _dot_nt(qh, k2)
            e = jnp.exp(s - jnp.max(s, axis=-1, keepdims=True))
            l = jnp.sum(e, axis=-1, keepdims=True)
            outs.append(_dot(e.astype(BF16), v2) / l)
        o2 = jnp.where(first, outs[0], outs[1])
        og_ref[:, sl] = (o2 * g_ref[:, sl].astype(F32)).astype(BF16)
    y_ref[...] = x_ref[...] + mod_ref[0][2:3] * _dot(og_ref[...], wo_ref[...])


def _na_prompt_attn(q, k, v, g, x, mod, w_out):
    t = x.shape[0]
    tm = TOKEN_BLOCK
    tok = pl.BlockSpec((tm, D_MODEL), lambda i: (i, 0))
    return pl.pallas_call(
        _na_prompt_attn_kernel,
        grid=(t // tm,),
        in_specs=[tok, tok, tok, tok, tok,
                  _const_spec((1, 3, D_MODEL)),
                  _const_spec((D_MODEL, D_MODEL))],
        out_specs=tok,
        out_shape=jax.ShapeDtypeStruct((t, D_MODEL), F32),
        scratch_shapes=[pltpu.VMEM((tm, D_MODEL), BF16)],
        compiler_params=_params(1),
        name="na_prompt_attn",
    )(q, k, v, g, x, mod, w_out)


def _na_latent_attn_kernel(q_ref, k_ref, v_ref, kc_ref, vc_ref, bias_ref, g_ref, x_ref, mod_ref,
                           wo_ref, y_ref, og_ref):
    i = pl.program_id(0)
    win0 = pl.multiple_of((i // 2) * NA_TQ, NA_TQ)
    first = _pair_masks()
    pw = 2 * NA_HEAD_DIM
    for p in range(NA_HEADS // 2):
        sl = slice(p * pw, (p + 1) * pw)
        q2 = q_ref[:, sl]
        kw = k_ref[0, pl.ds(win0, NA_WIN), sl]
        vw = v_ref[0, pl.ds(win0, NA_WIN), sl]
        kc_t, vc_t = kc_ref[0, sl, :], vc_ref[0, sl, :]
        outs = []
        for hh, qh in enumerate((jnp.where(first, q2, jnp.zeros_like(q2)),
                                 jnp.where(first, jnp.zeros_like(q2), q2))):
            s_loc = _dot_nt(qh, kw) + bias_ref[2 * p + hh, 0].astype(F32)
            s_ctx = _dot(qh, kc_t)
            m = jnp.maximum(jnp.max(s_loc, axis=-1, keepdims=True),
                            jnp.max(s_ctx, axis=-1, keepdims=True))
            e_loc = jnp.exp(s_loc - m)
            e_ctx = jnp.exp(s_ctx - m)
            l = jnp.sum(e_loc, axis=-1, keepdims=True) + jnp.sum(e_ctx, axis=-1, keepdims=True)
            outs.append((_dot(e_loc.astype(BF16), vw) + _dot_nt(e_ctx.astype(BF16), vc_t)) / l)
        o2 = jnp.where(first, outs[0], outs[1])
        og_ref[:, sl] = (o2 * g_ref[:, sl].astype(F32)).astype(BF16)
    y_ref[...] = x_ref[...] + mod_ref[0][2:3] * _dot(og_ref[...], wo_ref[...])


def _na_latent_attn(q, k, v, kc, vc, bias, g, x, mod, w_out):
    nb, n = k.shape[0], k.shape[1]
    nq = n // NA_TQ
    tok = pl.BlockSpec((NA_TQ, D_MODEL), lambda i, b: (b * nq + i, 0))
    full = pl.BlockSpec((1, n, D_MODEL), lambda i, b: (b, 0, 0))
    ctx = pl.BlockSpec((1, D_MODEL, kc.shape[2]), lambda i, b: (b, 0, 0))
    return pl.pallas_call(
        _na_latent_attn_kernel,
        grid=(nq, nb),
        in_specs=[tok, full, full, ctx, ctx,
                  pl.BlockSpec((NA_HEADS, 1, NA_TQ, NA_WIN), lambda i, b: (0, i, 0, 0)),
                  tok, tok,
                  pl.BlockSpec((1, 3, D_MODEL), lambda i, b: (b, 0, 0)),
                  _const_spec((D_MODEL, D_MODEL))],
        out_specs=tok,
        out_shape=jax.ShapeDtypeStruct((nb * n, D_MODEL), F32),
        scratch_shapes=[pltpu.VMEM((NA_TQ, D_MODEL), BF16)],
        compiler_params=_params(2),
        name="na_latent_attn",
    )(q, k, v, kc, vc, bias, g, x, mod, w_out)


def _na_window_plan(rows):
    n_blk = rows // NA_Q_ROWS
    kh = min(NA_KH, rows)
    n_chunk = NA_WIN_ROWS // 2
    a0 = np.zeros((n_blk, NA_Q_ROWS, n_chunk), np.int64)
    row_mask = np.zeros((n_blk, NA_TQ, NA_WIN), np.float32)
    for i in range(n_blk):
        win_row0 = (i // 2) * NA_Q_ROWS
        for ql in range(NA_Q_ROWS):
            qr = i * NA_Q_ROWS + ql
            rstart = min(max(qr - kh // 2, 0), rows - kh)
            for kl in range(NA_WIN_ROWS):
                kr = win_row0 + kl
                if not rstart <= kr < rstart + kh:
                    row_mask[i, ql * GRID_W:(ql + 1) * GRID_W, kl * GRID_W:(kl + 1) * GRID_W] = NEG_INF
            for m in range(n_chunk):
                a0[i, ql, m] = win_row0 + 2 * m - qr + (NA_KH - 1)
    return a0, row_mask


def _na_bias_kernel(t2_ref, rm_ref, o_ref, *, idx):
    n_blk, n_ql, n_chunk = idx.shape
    for i in range(n_blk):
        for ql in range(n_ql):
            rs = slice(ql * GRID_W, (ql + 1) * GRID_W)
            for m in range(n_chunk):
                cs = slice(m * 128, (m + 1) * 128)
                o_ref[0, i, rs, cs] = t2_ref[0, int(idx[i, ql, m])] + rm_ref[i, rs, cs]


def _na_bias_tables(rpb, rows):
    a0, row_mask = _na_window_plan(rows)
    n_blk = a0.shape[0]
    cq = np.arange(GRID_W)
    cstart = np.clip(cq - NA_KW // 2, 0, GRID_W - NA_KW)
    col_ok = (cq[None, :] >= cstart[:, None]) & (cq[None, :] < cstart[:, None] + NA_KW)
    coff = np.clip(cq[None, :] - cq[:, None], -(NA_KW - 1), NA_KW - 1) + (NA_KW - 1)
    onehot = (coff[None] == np.arange(2 * NA_KW - 1)[:, None, None]) & col_ok[None]
    t = jnp.einsum("hab,bqk->haqk", rpb, jnp.asarray(onehot, F32), precision=lax.Precision.HIGHEST)
    t = jnp.where(col_ok[None, None], t, NEG_INF)
    lo, hi = int(a0.min()), int(a0.max()) + 1
    n_off = 2 * NA_KH - 1
    tm = jnp.pad(t, ((0, 0), (max(-lo, 0), max(hi - n_off + 1, 0)), (0, 0), (0, 0)), constant_values=NEG_INF)
    n_pair = hi - lo
    t2 = jnp.concatenate([tm[:, 0:n_pair], tm[:, 1:n_pair + 1]], axis=-1).astype(BF16)
    return pl.pallas_call(
        functools.partial(_na_bias_kernel, idx=a0 - lo),
        grid=(NA_HEADS,),
        in_specs=[pl.BlockSpec((1, n_pair, GRID_W, 128), lambda h: (h, 0, 0, 0)),
                  _const_spec((n_blk, NA_TQ, NA_WIN))],
        out_specs=pl.BlockSpec((1, n_blk, NA_TQ, NA_WIN), lambda h: (h, 0, 0, 0)),
        out_shape=jax.ShapeDtypeStruct((NA_HEADS, n_blk, NA_TQ, NA_WIN), BF16),
        compiler_params=_params(1, 32 * 1024 * 1024),
        name="na_bias",
    )(t2, jnp.asarray(row_mask, BF16))


def _ret_proj_kernel(x_ref, mod_ref, nw_ref, w_ref, *rest, rope):
    if rope:
        cos_ref, sin_ref, q_out, k_out, v_out, g_out = rest
    else:
        q_out, k_out, v_out, g_out = rest
    h = _norm_mod(x_ref[...], nw_ref[...], mod_ref[0]).astype(BF16)
    qw, vw = RET_QK_WIDTH, RET_WIDTH

    def rot(t):
        if not rope:
            return t
        cos, sin = cos_ref[...], sin_ref[...]
        parts = []
        for c in range(qw // 128):
            sl = slice(c * 128, (c + 1) * 128)
            tc = t[:, sl]
            parts.append(tc * cos[:, sl] + pltpu.roll(tc, 64, axis=1) * sin[:, sl])
        return jnp.concatenate(parts, axis=1)

    q_out[...] = rot(_dot(h, w_ref[:, 0:qw])).astype(BF16)
    k_out[...] = (rot(_dot(h, w_ref[:, qw:2 * qw])) * (RET_QK_DIM ** -0.5)).astype(BF16)
    v_out[...] = _dot(h, w_ref[:, 2 * qw:2 * qw + vw]).astype(BF16)
    g_out[...] = _silu(_dot(h, w_ref[:, 2 * qw + vw:2 * qw + 2 * vw])).astype(BF16)


def _ret_proj(x, mod, nw, w_in, blocks_per_batch, rope_tables):
    t = x.shape[0]
    tm = TOKEN_BLOCK
    tok = pl.BlockSpec((tm, D_MODEL), lambda i: (i, 0))
    tok2 = pl.BlockSpec((tm, RET_WIDTH), lambda i: (i, 0))
    rope = rope_tables is not None
    in_specs = [tok,
                pl.BlockSpec((1, 3, D_MODEL), lambda i: (i // blocks_per_batch, 0, 0)),
                _const_spec((1, D_MODEL)),
                _const_spec((D_MODEL, 2 * RET_QK_WIDTH + 2 * RET_WIDTH))]
    args = [x, mod, nw, w_in]
    if rope:
        pos = pl.BlockSpec((tm, RET_QK_WIDTH), lambda i: (i % blocks_per_batch, 0))
        in_specs += [pos, pos]
        args += list(rope_tables)
    return pl.pallas_call(
        functools.partial(_ret_proj_kernel, rope=rope),
        grid=(t // tm,),
        in_specs=in_specs,
        out_specs=[tok, tok, tok2, tok2],
        out_shape=[jax.ShapeDtypeStruct((t, RET_QK_WIDTH), BF16)] * 2
        + [jax.ShapeDtypeStruct((t, RET_WIDTH), BF16)] * 2,
        compiler_params=_params(1),
        name="ret_proj",
    )(*args)


def _rope_tables(n):
    half = RET_QK_DIM // 2
    t = np.arange(n)
    inv = ROPE_BASE ** (-jnp.arange(0, half, 2, dtype=F32) / half)
    tabs = []
    for pos in ((t // GRID_W).astype(np.float32), (t % GRID_W).astype(np.float32)):
        ang = jnp.asarray(pos)[:, None] * inv[None, :]
        tabs.append((jnp.cos(ang), jnp.sin(ang)))
    cos = jnp.concatenate([tabs[0][0], tabs[0][0], tabs[1][0], tabs[1][0]], axis=1)
    sin = jnp.concatenate([-tabs[0][1], tabs[0][1], -tabs[1][1], tabs[1][1]], axis=1)
    return jnp.tile(cos, (1, RET_HEADS)), jnp.tile(sin, (1, RET_HEADS))


def _ret_core_kernel(lg_ref, q_ref, k_ref, v_ref, g_ref, x_ref, mod_ref, gn_ref, wo_ref, *rest,
                     n, has_s0, emit_state):
    rest = list(rest)
    s0_ref = rest.pop(0) if has_s0 else None
    y_ref = rest.pop(0)
    st_ref = rest.pop(0) if emit_state else None
    yg_ref = rest.pop(0)
    tq = q_ref.shape[0]
    row0 = pl.program_id(1) * tq
    qi = (row0 + lax.broadcasted_iota(jnp.int32, (tq, n), 0)).astype(F32)
    kj = lax.broadcasted_iota(jnp.int32, (tq, n), 1).astype(F32)
    diff = qi - kj
    for h in range(RET_HEADS):
        lgf, lgb = lg_ref[h], lg_ref[RET_HEADS + h]
        qh = q_ref[:, h * RET_QK_DIM:(h + 1) * RET_QK_DIM]
        kh = k_ref[0, :, h * RET_QK_DIM:(h + 1) * RET_QK_DIM]
        vh = v_ref[0, :, h * RET_V_DIM:(h + 1) * RET_V_DIM]
        dec = jnp.exp(jnp.where(diff >= 0.0, lgf * diff, -lgb * diff))
        dec = jnp.where(diff == 0.0, 2.0, dec)
        inner = (_dot_nt(qh, kh) * dec).astype(BF16)
        o = _dot(inner, vh)
        if has_s0:
            pos = qi[:, 0:1]
            o = o + _dot(qh, s0_ref[0, 0, h]) * jnp.exp(lgf * (pos + 1.0))
            o = o + _dot(qh, s0_ref[0, 1, h]) * jnp.exp(lgb * (float(n) - pos))
        if emit_state:
            kpos = lax.broadcasted_iota(jnp.int32, (n, RET_QK_DIM), 0).astype(F32)
            kf = kh.astype(F32)
            kdf = (kf * jnp.exp(lgf * (float(n - 1) - kpos))).T.astype(BF16)
            kdb = (kf * jnp.exp(lgb * kpos)).T.astype(BF16)
            st_ref[0, 0, h] = _dot(kdf, vh)
            st_ref[0, 1, h] = _dot(kdb, vh)
        mu = jnp.mean(o, axis=-1, keepdims=True)
        oc = o - mu
        var = jnp.mean(oc * oc, axis=-1, keepdims=True)
        vs = slice(h * RET_V_DIM, (h + 1) * RET_V_DIM)
        yn = oc * lax.rsqrt(var + EPS) * gn_ref[:, vs]
        yg_ref[:, vs] = (yn * g_ref[:, vs].astype(F32)).astype(BF16)
    y_ref[...] = x_ref[...] + mod_ref[0][2:3] * _dot(yg_ref[...], wo_ref[...])


def _ret_core(lg, q, k, v, g, x, mod, gn_w, w_out, n, mod_per_batch, s0=None, emit_state=False):
    t = x.shape[0]
    nb = t // n
    tq = TOKEN_BLOCK
    nq = n // tq
    tok = pl.BlockSpec((tq, D_MODEL), lambda b, i: (b * nq + i, 0))
    tok2 = pl.BlockSpec((tq, RET_WIDTH), lambda b, i: (b * nq + i, 0))
    in_specs = [pl.BlockSpec(memory_space=pltpu.SMEM),
                tok,
                pl.BlockSpec((1, n, RET_QK_WIDTH), lambda b, i: (b, 0, 0)),
                pl.BlockSpec((1, n, RET_WIDTH), lambda b, i: (b, 0, 0)),
                tok2, tok,
                pl.BlockSpec((1, 3, D_MODEL), (lambda b, i: (b, 0, 0)) if mod_per_batch else (lambda b, i: (0, 0, 0))),
                _const_spec((1, RET_WIDTH)),
                _const_spec((RET_WIDTH, D_MODEL))]
    args = [lg, q, k.reshape(nb, n, RET_QK_WIDTH), v.reshape(nb, n, RET_WIDTH), g, x, mod, gn_w, w_out]
    st_block = (1, 2, RET_HEADS, RET_QK_DIM, RET_V_DIM)
    if s0 is not None:
        in_specs.append(pl.BlockSpec(st_block, lambda b, i: (b, 0, 0, 0, 0)))
        args.append(s0)
    out_specs = [tok]
    out_shape = [jax.ShapeDtypeStruct((t, D_MODEL), F32)]
    if emit_state:
        assert nq == 1
        out_specs.append(pl.BlockSpec(st_block, lambda b, i: (b, 0, 0, 0, 0)))
        out_shape.append(jax.ShapeDtypeStruct((nb,) + st_block[1:], F32))
    return pl.pallas_call(
        functools.partial(_ret_core_kernel, n=n, has_s0=s0 is not None, emit_state=emit_state),
        grid=(nb, nq),
        in_specs=in_specs,
        out_specs=out_specs,
        out_shape=out_shape,
        scratch_shapes=[pltpu.VMEM((tq, RET_WIDTH), BF16)],
        compiler_params=_params(2),
        name="ret_core",
    )(*args)


def _mlp_kernel(x_ref, mod_ref, nw_ref, wi_ref, lnw_ref, lnb_ref, ws_ref, bs_ref, wo_ref, y_ref, o_ref):
    x = x_ref[...]
    mod = mod_ref[0]
    h = _norm_mod(x, nw_ref[...], mod).astype(BF16)
    w = MLP_WIDTH
    v = _gelu_tanh(_dot(h, wi_ref[:, w:2 * w]))
    mu = jnp.mean(v, axis=-1, keepdims=True)
    vc = v - mu
    var = jnp.mean(vc * vc, axis=-1, keepdims=True)
    vn = (vc * lax.rsqrt(var + EPS) * lnw_ref[...] + lnb_ref[...]).astype(BF16)
    u = _gelu_tanh(_dot(h, wi_ref[:, 0:w]))
    ug = u * _silu(_dot(h, wi_ref[:, 2 * w:3 * w]))
    tm = x.shape[0]
    for c in range(tm // MLP_CHUNK):
        rs = slice(c * MLP_CHUNK, (c + 1) * MLP_CHUNK)
        for g in range(MLP_GROUPS):
            cs = slice(g * MLP_GROUP_DIM, (g + 1) * MLP_GROUP_DIM)
            sv = _dot(ws_ref[g], vn[rs, cs]) + bs_ref[:, cs]
            o_ref[rs, cs] = (ug[rs, cs] * sv).astype(BF16)
    y_ref[...] = x + mod[2:3] * _dot(o_ref[...], wo_ref[...])


def _mlp_layer(x, mod, nw, w_in, ln_w, ln_b, w_s, b_s_cols, w_out, blocks_per_batch):
    t = x.shape[0]
    tm = TOKEN_BLOCK
    tok = pl.BlockSpec((tm, D_MODEL), lambda i: (i, 0))
    return pl.pallas_call(
        _mlp_kernel,
        grid=(t // tm,),
        in_specs=[tok,
                  pl.BlockSpec((1, 3, D_MODEL), lambda i: (i // blocks_per_batch, 0, 0)),
                  _const_spec((1, D_MODEL)),
                  _const_spec((D_MODEL, 3 * MLP_WIDTH)),
                  _const_spec((1, MLP_WIDTH)),
                  _const_spec((1, MLP_WIDTH)),
                  _const_spec((MLP_GROUPS, MLP_CHUNK, MLP_CHUNK)),
                  _const_spec((MLP_CHUNK, MLP_WIDTH)),
                  _const_spec((MLP_WIDTH, D_MODEL))],
        out_specs=tok,
        out_shape=jax.ShapeDtypeStruct((t, D_MODEL), F32),
        scratch_shapes=[pltpu.VMEM((tm, MLP_WIDTH), BF16)],
        compiler_params=_params(1),
        name="mlp_layer",
    )(x, mod, nw, w_in, ln_w, ln_b, w_s, b_s_cols, w_out)


def kernel(x_prompt, x_sample, cache_na_k, cache_na_v, state_ret, c, c_ctx, norm_w, w_ada, b_ada,
           na_w_in, na_w_out, na_q_gain, na_k_gain, na_rpb,
           ret_w_in, ret_w_out, ret_decay_logit, ret_gn_w,
           mlp_w_in, mlp_ln_w, mlp_ln_b, mlp_w_s, mlp_b_s, mlp_w_out):
    batch, seq, _ = x_prompt.shape
    dec_batch, dec_seq, _ = x_sample.shape
    past = cache_na_k.shape[2]
    assert seq == TOKEN_BLOCK and dec_seq == 16 * GRID_W and dec_batch <= 7

    c_rows = jnp.zeros((8, D_MODEL), F32).at[:dec_batch].set(c).at[dec_batch].set(c_ctx)
    mods = _ada_all(c_rows, w_ada, b_ada).reshape(DEPTH, 8, 3, D_MODEL)

    lane_head = np.arange(D_MODEL) // NA_HEAD_DIM
    bd = jnp.asarray(lane_head[:, None] == lane_head[None, :], dtype=BF16)
    rope_tabs = _rope_tables(dec_seq)

    yp = x_prompt.reshape(batch * seq, D_MODEL)
    ys = x_sample.reshape(dec_batch * dec_seq, D_MODEL)
    bpb_s = dec_seq // TOKEN_BLOCK
    n_na = (DEPTH + 2) // N_MIXERS
    new_kv, new_s = None, []
    for i in range(DEPTH):
        kind, j = i % N_MIXERS, i // N_MIXERS
        mod_p = mods[i, dec_batch:dec_batch + 1]
        mod_s = mods[i, :dec_batch]
        nw = norm_w[i].reshape(1, D_MODEL)
        if kind == 0:
            w_in = na_w_in[j].astype(BF16)
            w_out = na_w_out[j].astype(BF16)
            qg = jnp.tile(na_q_gain[j], NA_HEADS).reshape(1, D_MODEL)
            kg = jnp.tile(na_k_gain[j], NA_HEADS).reshape(1, D_MODEL)
            q, k, v, g, *new_kv = _na_proj(yp, mod_p, nw, w_in, qg, kg, bd, batch,
                                           cache_slot=(n_na, j), cache_prev=new_kv)
            yp = _na_prompt_attn(q, k, v, g, yp, mod_p, w_out)
            q, k, v, g = _na_proj(ys, mod_s, nw, w_in, qg, kg, bd, bpb_s)
            kc = cache_na_k[:, j].transpose(0, 2, 3, 1).reshape(dec_batch, D_MODEL, past).astype(BF16)
            vc = cache_na_v[:, j].transpose(0, 2, 3, 1).reshape(dec_batch, D_MODEL, past).astype(BF16)
            ys = _na_latent_attn(q, k.reshape(dec_batch, dec_seq, D_MODEL),
                                 v.reshape(dec_batch, dec_seq, D_MODEL), kc, vc,
                                 _na_bias_tables(na_rpb[j], dec_seq // GRID_W), g, ys, mod_s, w_out)
        elif kind == 1:
            w_in = ret_w_in[j].astype(BF16)
            w_out = ret_w_out[j].astype(BF16)
            lg = jax.nn.log_sigmoid(ret_decay_logit[j].astype(F32)).reshape(2 * RET_HEADS)
            gn = ret_gn_w[j].reshape(1, RET_WIDTH)
            q, k, v, g = _ret_proj(yp, mod_p, nw, w_in, batch, None)
            yp, st = _ret_core(lg, q, k, v, g, yp, mod_p, gn, w_out, seq, False, emit_state=True)
            new_s.append(st)
            q, k, v, g = _ret_proj(ys, mod_s, nw, w_in, bpb_s, rope_tabs)
            (ys,) = _ret_core(lg, q, k, v, g, ys, mod_s, gn, w_out, dec_seq, True,
                              s0=state_ret[:, j].astype(BF16))
        else:
            w_in = mlp_w_in[j].astype(BF16)
            w_out = mlp_w_out[j].astype(BF16)
            lnw = mlp_ln_w[j].reshape(1, MLP_WIDTH)
            lnb = mlp_ln_b[j].reshape(1, MLP_WIDTH)
            w_s = mlp_w_s[j].astype(BF16)
            b_cols = jnp.repeat(mlp_b_s[j].T, MLP_GROUP_DIM, axis=1)
            yp = _mlp_layer(yp, mod_p, nw, w_in, lnw, lnb, w_s, b_cols, w_out, batch)
            ys = _mlp_layer(ys, mod_s, nw, w_in, lnw, lnb, w_s, b_cols, w_out, bpb_s)
    return (yp.reshape(batch, seq, D_MODEL),
            ys.reshape(dec_batch, dec_seq, D_MODEL),
            *(t.reshape(batch, n_na, NA_HEADS, NA_HEAD_DIM, seq).transpose(0, 1, 4, 2, 3) for t in new_kv),
            jnp.stack(new_s, axis=1))
```

```python
import functools
import math

import numpy as np
import jax
import jax.numpy as jnp
from jax import lax
from jax.experimental import pallas as pl
from jax.experimental.pallas import tpu as pltpu

F32 = jnp.float32
BF16 = jnp.bfloat16

D_MODEL = 1024
DEPTH = 4
N_MIXERS = 3
GRID_W = 64
EPS = 1e-6
NEG_INF = -1e30
LOG2E = 1.4426950408889634

NA_HEADS = 16
NA_HEAD_DIM = 64
NA_KH = 8
NA_KW = 16
NA_Q_ROWS = 4
NA_WIN_ROWS = 12
NA_TQ = NA_Q_ROWS * GRID_W
NA_WIN = NA_WIN_ROWS * GRID_W

RET_HEADS = 4
RET_QK_DIM = 256
RET_V_DIM = 512
RET_QK_WIDTH = RET_HEADS * RET_QK_DIM
RET_WIDTH = RET_HEADS * RET_V_DIM
ROPE_BASE = 10000.0

MLP_WIDTH = 2048
MLP_GROUPS = 8
MLP_GROUP_DIM = 256
MLP_CHUNK = 128

TOKEN_BLOCK = 256
V7X_VMEM_LIMIT = 56 * 1024 * 1024

_NT = (((1,), (1,)), ((), ()))
_TN = (((0,), (0,)), ((), ()))


def _dot(a, b):
    return jnp.dot(a, b, preferred_element_type=F32)


def _dot_nt(a, b):
    return lax.dot_general(a, b, _NT, preferred_element_type=F32)


def _silu(x):
    return x * jax.nn.sigmoid(x)


def _gelu_tanh(x):
    c = math.sqrt(2.0 / math.pi)
    return x * (0.5 * (1.0 + jnp.tanh(c * (x + 0.044715 * (x * x * x)))))


def _norm_mod(x, nw, mod):
    ms = jnp.mean(x * x, axis=-1, keepdims=True)
    y = x * lax.rsqrt(ms + EPS) * nw
    return y * (1.0 + mod[1:2]) + mod[0:1]


def _const_spec(shape):
    nd = len(shape)
    return pl.BlockSpec(shape, lambda *_: (0,) * nd, pipeline_mode=pl.Buffered(1))


def _params(n_axes, vmem=V7X_VMEM_LIMIT):
    return pltpu.CompilerParams(dimension_semantics=("arbitrary",) * n_axes,
                                vmem_limit_bytes=vmem)


def _ada_kernel(c_ref, w_ref, b_ref, o_ref):
    a = _silu(c_ref[...]).astype(BF16)
    o_ref[0] = _dot(a, w_ref[0].astype(BF16)) + b_ref[0]


def _ada_all(c_rows, w_ada, b_ada):
    n_col = 3
    return pl.pallas_call(
        _ada_kernel,
        grid=(DEPTH, n_col),
        in_specs=[
            pl.BlockSpec((8, D_MODEL), lambda i, n: (0, 0)),
            pl.BlockSpec((1, D_MODEL, D_MODEL), lambda i, n: (i, 0, n)),
            pl.BlockSpec((1, 1, D_MODEL), lambda i, n: (i, 0, n)),
        ],
        out_specs=pl.BlockSpec((1, 8, D_MODEL), lambda i, n: (i, 0, n)),
        out_shape=jax.ShapeDtypeStruct((DEPTH, 8, 3 * D_MODEL), F32),
        compiler_params=_params(2, 32 * 1024 * 1024),
        name="adaln",
    )(c_rows, w_ada, b_ada.reshape(DEPTH, 1, 3 * D_MODEL))


def _na_proj_kernel(x_ref, mod_ref, nw_ref, w_ref, qg_ref, kg_ref, hsum_ref, hexp_ref, *rest,
                    emit_f32, n_alias):
    outs = rest[n_alias:]
    q_out, k_out, v_out, g_out = outs[:4]
    h = _norm_mod(x_ref[...], nw_ref[...], mod_ref[0]).astype(BF16)

    def head_norm(t, gain):
        ssq = _dot((t * t).astype(BF16), hsum_ref[...])
        r = lax.rsqrt(ssq * (1.0 / NA_HEAD_DIM) + EPS)
        hi = r.astype(BF16)
        lo = (r - hi.astype(F32)).astype(BF16)
        return t * _dot(jnp.concatenate([hi, lo], axis=1), hexp_ref[...]) * gain

    w = D_MODEL
    qn = head_norm(_dot(h, w_ref[:, 0:w]), qg_ref[...])
    q_out[...] = (qn * (LOG2E * NA_HEAD_DIM ** -0.5)).astype(BF16)
    kn = head_norm(_dot(h, w_ref[:, w:2 * w]), kg_ref[...])
    k_out[...] = kn.astype(BF16)
    v = _dot(h, w_ref[:, 2 * w:3 * w])
    vb = v.astype(BF16)
    ones = jnp.ones((vb.shape[0], 2 * NA_HEAD_DIM), BF16)
    for p in range(NA_HEADS // 2):
        v_out[:, 4 * p * NA_HEAD_DIM:(4 * p + 2) * NA_HEAD_DIM] = vb[:, 2 * p * NA_HEAD_DIM:(2 * p + 2) * NA_HEAD_DIM]
        v_out[:, (4 * p + 2) * NA_HEAD_DIM:(4 * p + 4) * NA_HEAD_DIM] = ones
    g_out[...] = _silu(_dot(h, w_ref[:, 3 * w:4 * w])).astype(BF16)
    if emit_f32:
        outs[4][0, 0] = kn.T
        outs[5][0, 0] = v.T


def _head_sum_tables():
    lane_head = np.arange(D_MODEL) // NA_HEAD_DIM
    hsum = (lane_head[:, None] == np.arange(128)[None, :]).astype(np.float32)
    return jnp.asarray(hsum, BF16), jnp.asarray(np.concatenate([hsum.T, hsum.T], axis=0), BF16)


def _na_proj(x, mod, nw, w_in, q_gain, k_gain, head_tabs, blocks_per_batch, cache_slot=None, cache_prev=None):
    t = x.shape[0]
    tm = TOKEN_BLOCK
    tok = pl.BlockSpec((tm, D_MODEL), lambda i: (i, 0))
    in_specs = [
        tok,
        pl.BlockSpec((1, 3, D_MODEL), lambda i: (i // blocks_per_batch, 0, 0)),
        _const_spec((1, D_MODEL)),
        _const_spec((D_MODEL, 4 * D_MODEL)),
        _const_spec((1, D_MODEL)),
        _const_spec((1, D_MODEL)),
        _const_spec((D_MODEL, 128)),
        _const_spec((256, D_MODEL)),
    ]
    args = [x, mod, nw, w_in, q_gain, k_gain, *head_tabs]
    tok_v = pl.BlockSpec((tm, 2 * D_MODEL), lambda i: (i, 0))
    out_specs = [tok, tok, tok_v, tok]
    out_shape = [jax.ShapeDtypeStruct((t, w), BF16) for w in (D_MODEL, D_MODEL, 2 * D_MODEL, D_MODEL)]
    aliases = {}
    if cache_slot is not None:
        n_layers, j = cache_slot
        out_specs += [pl.BlockSpec((1, 1, D_MODEL, tm), lambda i: (i, j, 0, 0))] * 2
        out_shape += [jax.ShapeDtypeStruct((t // tm, n_layers, D_MODEL, tm), F32)] * 2
        if cache_prev is not None:
            aliases = {len(args): 4, len(args) + 1: 5}
            in_specs += [pl.BlockSpec(memory_space=pl.ANY)] * 2
            args += list(cache_prev)
    return pl.pallas_call(
        functools.partial(_na_proj_kernel, emit_f32=cache_slot is not None, n_alias=len(aliases)),
        grid=(t // tm,),
        in_specs=in_specs,
        out_specs=out_specs,
        out_shape=out_shape,
        input_output_aliases=aliases,
        compiler_params=_params(1),
        name="na_proj",
    )(*args)


def _pair_masks():
    lane = lax.broadcasted_iota(jnp.int32, (1, 2 * NA_HEAD_DIM), 1)
    return lane < NA_HEAD_DIM


def _softmax_pv(s_parts, v_parts):
    m = functools.reduce(jnp.maximum, [jnp.max(s, axis=-1, keepdims=True) for s in s_parts])
    acc = None
    for s, (v, v_is_t) in zip(s_parts, v_parts):
        e = jnp.exp2(s - m).astype(BF16)
        o = _dot_nt(e, v) if v_is_t else _dot(e, v)
        acc = o if acc is None else acc + o
    pw = 2 * NA_HEAD_DIM
    return acc[:, :pw] / acc[:, pw:]


def _na_prompt_attn_kernel(q_ref, k_ref, v_ref, g_ref, x_ref, mod_ref, wo_ref, y_ref, og_ref):
    first = _pair_masks()
    pw = 2 * NA_HEAD_DIM
    for p in range(NA_HEADS // 2):
        sl = slice(p * pw, (p + 1) * pw)
        q2, k2 = q_ref[:, sl], k_ref[:, sl]
        v2 = v_ref[:, 2 * p * pw:2 * (p + 1) * pw]
        outs = []
        for qh in (jnp.where(first, q2, jnp.zeros_like(q2)), jnp.where(first, jnp.zeros_like(q2), q2)):
            outs.append(_softmax_pv([_dot_nt(qh, k2)], [(v2, False)]))
        o2 = jnp.where(first, outs[0], outs[1])
        og_ref[:, sl] = (o2 * g_ref[:, sl].astype(F32)).astype(BF16)
    y_ref[...] = x_ref[...] + mod_ref[0][2:3] * _dot(og_ref[...], wo_ref[...])


def _na_prompt_attn(q, k, v, g, x, mod, w_out):
    t = x.shape[0]
    tm = TOKEN_BLOCK
    tok = pl.BlockSpec((tm, D_MODEL), lambda i: (i, 0))
    return pl.pallas_call(
        _na_prompt_attn_kernel,
        grid=(t // tm,),
        in_specs=[tok, tok, pl.BlockSpec((tm, 2 * D_MODEL), lambda i: (i, 0)), tok, tok,
                  _const_spec((1, 3, D_MODEL)),
                  _const_spec((D_MODEL, D_MODEL))],
        out_specs=tok,
        out_shape=jax.ShapeDtypeStruct((t, D_MODEL), F32),
        scratch_shapes=[pltpu.VMEM((tm, D_MODEL), BF16)],
        compiler_params=_params(1),
        name="na_prompt_attn",
    )(q, k, v, g, x, mod, w_out)


def _na_latent_attn_kernel(q_ref, k_ref, v_ref, kc_ref, vc_ref, bias_ref, g_ref, x_ref, mod_ref,
                           wo_ref, y_ref, og_ref):
    i = pl.program_id(0)
    win0 = pl.multiple_of((i // 2) * NA_TQ, NA_TQ)
    first = _pair_masks()
    pw = 2 * NA_HEAD_DIM
    for p in range(NA_HEADS // 2):
        sl = slice(p * pw, (p + 1) * pw)
        q2 = q_ref[:, sl]
        sl2 = slice(2 * p * pw, 2 * (p + 1) * pw)
        kw = k_ref[0, pl.ds(win0, NA_WIN), sl]
        vw = v_ref[0, pl.ds(win0, NA_WIN), sl2]
        kc_t, vc_t = kc_ref[0, sl, :], vc_ref[0, sl2, :]
        outs = []
        for hh, qh in enumerate((jnp.where(first, q2, jnp.zeros_like(q2)),
                                 jnp.where(first, jnp.zeros_like(q2), q2))):
            s_loc = _dot_nt(qh, kw) + bias_ref[2 * p + hh, 0].astype(F32)
            outs.append(_softmax_pv([s_loc, _dot(qh, kc_t)], [(vw, False), (vc_t, True)]))
        o2 = jnp.where(first, outs[0], outs[1])
        og_ref[:, sl] = (o2 * g_ref[:, sl].astype(F32)).astype(BF16)
    y_ref[...] = x_ref[...] + mod_ref[0][2:3] * _dot(og_ref[...], wo_ref[...])


def _na_latent_attn(q, k, v, kc, vc, bias, g, x, mod, w_out):
    nb, n = k.shape[0], k.shape[1]
    nq = n // NA_TQ
    tok = pl.BlockSpec((NA_TQ, D_MODEL), lambda i, b: (b * nq + i, 0))
    full = pl.BlockSpec((1, n, D_MODEL), lambda i, b: (b, 0, 0))
    full2 = pl.BlockSpec((1, n, 2 * D_MODEL), lambda i, b: (b, 0, 0))
    ctx = pl.BlockSpec((1, D_MODEL, kc.shape[2]), lambda i, b: (b, 0, 0))
    ctx2 = pl.BlockSpec((1, 2 * D_MODEL, kc.shape[2]), lambda i, b: (b, 0, 0))
    return pl.pallas_call(
        _na_latent_attn_kernel,
        grid=(nq, nb),
        in_specs=[tok, full, full2, ctx, ctx2,
                  pl.BlockSpec((NA_HEADS, 1, NA_TQ, NA_WIN), lambda i, b: (0, i, 0, 0)),
                  tok, tok,
                  pl.BlockSpec((1, 3, D_MODEL), lambda i, b: (b, 0, 0)),
                  _const_spec((D_MODEL, D_MODEL))],
        out_specs=tok,
        out_shape=jax.ShapeDtypeStruct((nb * n, D_MODEL), F32),
        scratch_shapes=[pltpu.VMEM((NA_TQ, D_MODEL), BF16)],
        compiler_params=_params(2),
        name="na_latent_attn",
    )(q, k, v, kc, vc, bias, g, x, mod, w_out)


def _na_window_plan(rows):
    n_blk = rows // NA_Q_ROWS
    kh = min(NA_KH, rows)
    n_chunk = NA_WIN_ROWS // 2
    a0 = np.zeros((n_blk, NA_Q_ROWS, n_chunk), np.int64)
    row_mask = np.zeros((n_blk, NA_TQ, NA_WIN), np.float32)
    for i in range(n_blk):
        win_row0 = (i // 2) * NA_Q_ROWS
        for ql in range(NA_Q_ROWS):
            qr = i * NA_Q_ROWS + ql
            rstart = min(max(qr - kh // 2, 0), rows - kh)
            for kl in range(NA_WIN_ROWS):
                kr = win_row0 + kl
                if not rstart <= kr < rstart + kh:
                    row_mask[i, ql * GRID_W:(ql + 1) * GRID_W, kl * GRID_W:(kl + 1) * GRID_W] = NEG_INF
            for m in range(n_chunk):
                a0[i, ql, m] = win_row0 + 2 * m - qr + (NA_KH - 1)
    return a0, row_mask


def _na_bias_kernel(t2_ref, rm_ref, o_ref, *, idx):
    n_blk, n_ql, n_chunk = idx.shape
    for i in range(n_blk):
        for ql in range(n_ql):
            rs = slice(ql * GRID_W, (ql + 1) * GRID_W)
            for m in range(n_chunk):
                cs = slice(m * 128, (m + 1) * 128)
                o_ref[0, i, rs, cs] = t2_ref[0, int(idx[i, ql, m])] + rm_ref[i, rs, cs]


def _na_bias_tables(rpb, rows):
    a0, row_mask = _na_window_plan(rows)
    n_blk = a0.shape[0]
    cq = np.arange(GRID_W)
    cstart = np.clip(cq - NA_KW // 2, 0, GRID_W - NA_KW)
    col_ok = (cq[None, :] >= cstart[:, None]) & (cq[None, :] < cstart[:, None] + NA_KW)
    coff = np.clip(cq[None, :] - cq[:, None], -(NA_KW - 1), NA_KW - 1) + (NA_KW - 1)
    onehot = (coff[None] == np.arange(2 * NA_KW - 1)[:, None, None]) & col_ok[None]
    t = jnp.einsum("hab,bqk->haqk", rpb, jnp.asarray(onehot, F32), precision=lax.Precision.HIGHEST)
    t = jnp.where(col_ok[None, None], t * LOG2E, NEG_INF)
    lo, hi = int(a0.min()), int(a0.max()) + 1
    n_off = 2 * NA_KH - 1
    tm = jnp.pad(t, ((0, 0), (max(-lo, 0), max(hi - n_off + 1, 0)), (0, 0), (0, 0)), constant_values=NEG_INF)
    n_pair = hi - lo
    t2 = jnp.concatenate([tm[:, 0:n_pair], tm[:, 1:n_pair + 1]], axis=-1).astype(BF16)
    return pl.pallas_call(
        functools.partial(_na_bias_kernel, idx=a0 - lo),
        grid=(NA_HEADS,),
        in_specs=[pl.BlockSpec((1, n_pair, GRID_W, 128), lambda h: (h, 0, 0, 0)),
                  _const_spec((n_blk, NA_TQ, NA_WIN))],
        out_specs=pl.BlockSpec((1, n_blk, NA_TQ, NA_WIN), lambda h: (h, 0, 0, 0)),
        out_shape=jax.ShapeDtypeStruct((NA_HEADS, n_blk, NA_TQ, NA_WIN), BF16),
        compiler_params=_params(1, 32 * 1024 * 1024),
        name="na_bias",
    )(t2, jnp.asarray(row_mask, BF16))


def _ret_proj_kernel(x_ref, mod_ref, nw_ref, w_ref, *rest, rope):
    if rope:
        cos_ref, sin_ref, q_out, k_out, v_out, g_out = rest
    else:
        q_out, k_out, v_out, g_out = rest
    h = _norm_mod(x_ref[...], nw_ref[...], mod_ref[0]).astype(BF16)
    qw, vw = RET_QK_WIDTH, RET_WIDTH

    def rot(t):
        if not rope:
            return t
        cos, sin = cos_ref[...], sin_ref[...]
        parts = []
        for c in range(qw // 128):
            sl = slice(c * 128, (c + 1) * 128)
            tc = t[:, sl]
            parts.append(tc * cos[:, sl] + pltpu.roll(tc, 64, axis=1) * sin[:, sl])
        return jnp.concatenate(parts, axis=1)

    q_out[...] = rot(_dot(h, w_ref[:, 0:qw])).astype(BF16)
    k_out[...] = (rot(_dot(h, w_ref[:, qw:2 * qw])) * (RET_QK_DIM ** -0.5)).astype(BF16)
    v_out[...] = _dot(h, w_ref[:, 2 * qw:2 * qw + vw]).astype(BF16)
    g_out[...] = _silu(_dot(h, w_ref[:, 2 * qw + vw:2 * qw + 2 * vw])).astype(BF16)


def _ret_proj(x, mod, nw, w_in, blocks_per_batch, rope_tables):
    t = x.shape[0]
    tm = TOKEN_BLOCK
    tok = pl.BlockSpec((tm, D_MODEL), lambda i: (i, 0))
    tok2 = pl.BlockSpec((tm, RET_WIDTH), lambda i: (i, 0))
    rope = rope_tables is not None
    in_specs = [tok,
                pl.BlockSpec((1, 3, D_MODEL), lambda i: (i // blocks_per_batch, 0, 0)),
                _const_spec((1, D_MODEL)),
                _const_spec((D_MODEL, 2 * RET_QK_WIDTH + 2 * RET_WIDTH))]
    args = [x, mod, nw, w_in]
    if rope:
        pos = pl.BlockSpec((tm, RET_QK_WIDTH), lambda i: (i % blocks_per_batch, 0))
        in_specs += [pos, pos]
        args += list(rope_tables)
    return pl.pallas_call(
        functools.partial(_ret_proj_kernel, rope=rope),
        grid=(t // tm,),
        in_specs=in_specs,
        out_specs=[tok, tok, tok2, tok2],
        out_shape=[jax.ShapeDtypeStruct((t, RET_QK_WIDTH), BF16)] * 2
        + [jax.ShapeDtypeStruct((t, RET_WIDTH), BF16)] * 2,
        compiler_params=_params(1),
        name="ret_proj",
    )(*args)


def _rope_tables(n):
    half = RET_QK_DIM // 2
    t = np.arange(n)
    inv = ROPE_BASE ** (-jnp.arange(0, half, 2, dtype=F32) / half)
    tabs = []
    for pos in ((t // GRID_W).astype(np.float32), (t % GRID_W).astype(np.float32)):
        ang = jnp.asarray(pos)[:, None] * inv[None, :]
        tabs.append((jnp.cos(ang), jnp.sin(ang)))
    cos = jnp.concatenate([tabs[0][0], tabs[0][0], tabs[1][0], tabs[1][0]], axis=1)
    sin = jnp.concatenate([-tabs[0][1], tabs[0][1], -tabs[1][1], tabs[1][1]], axis=1)
    return jnp.tile(cos, (1, RET_HEADS)), jnp.tile(sin, (1, RET_HEADS))


def _ret_core_kernel(lg_ref, q_ref, k_ref, v_ref, g_ref, x_ref, mod_ref, gn_ref, wo_ref, *rest,
                     n, has_s0, emit_state):
    rest = list(rest)
    s0_ref = rest.pop(0) if has_s0 else None
    y_ref = rest.pop(0)
    st_ref = rest.pop(0) if emit_state else None
    yg_ref = rest.pop(0)
    tq = q_ref.shape[0]
    row0 = pl.program_id(1) * tq
    qi = (row0 + lax.broadcasted_iota(jnp.int32, (tq, n), 0)).astype(F32)
    kj = lax.broadcasted_iota(jnp.int32, (tq, n), 1).astype(F32)
    diff = qi - kj
    for h in range(RET_HEADS):
        lgf, lgb = lg_ref[h], lg_ref[RET_HEADS + h]
        qh = q_ref[:, h * RET_QK_DIM:(h + 1) * RET_QK_DIM]
        kh = k_ref[0, :, h * RET_QK_DIM:(h + 1) * RET_QK_DIM]
        vh = v_ref[0, :, h * RET_V_DIM:(h + 1) * RET_V_DIM]
        dec = jnp.exp(jnp.where(diff >= 0.0, lgf * diff, -lgb * diff))
        dec = jnp.where(diff == 0.0, 2.0, dec)
        inner = (_dot_nt(qh, kh) * dec).astype(BF16)
        o = _dot(inner, vh)
        if has_s0:
            pos = qi[:, 0:1]
            o = o + _dot(qh, s0_ref[0, 0, h]) * jnp.exp(lgf * (pos + 1.0))
            o = o + _dot(qh, s0_ref[0, 1, h]) * jnp.exp(lgb * (float(n) - pos))
        if emit_state:
            kpos = lax.broadcasted_iota(jnp.int32, (n, RET_QK_DIM), 0).astype(F32)
            kf = kh.astype(F32)
            kdf = (kf * jnp.exp(lgf * (float(n - 1) - kpos))).T.astype(BF16)
            kdb = (kf * jnp.exp(lgb * kpos)).T.astype(BF16)
            st_ref[0, 0, h] = _dot(kdf, vh)
            st_ref[0, 1, h] = _dot(kdb, vh)
        mu = jnp.mean(o, axis=-1, keepdims=True)
        oc = o - mu
        var = jnp.mean(oc * oc, axis=-1, keepdims=True)
        vs = slice(h * RET_V_DIM, (h + 1) * RET_V_DIM)
        yn = oc * lax.rsqrt(var + EPS) * gn_ref[:, vs]
        yg_ref[:, vs] = (yn * g_ref[:, vs].astype(F32)).astype(BF16)
    y_ref[...] = x_ref[...] + mod_ref[0][2:3] * _dot(yg_ref[...], wo_ref[...])


def _ret_core(lg, q, k, v, g, x, mod, gn_w, w_out, n, mod_per_batch, s0=None, emit_state=False):
    t = x.shape[0]
    nb = t // n
    tq = TOKEN_BLOCK
    nq = n // tq
    tok = pl.BlockSpec((tq, D_MODEL), lambda b, i: (b * nq + i, 0))
    tok2 = pl.BlockSpec((tq, RET_WIDTH), lambda b, i: (b * nq + i, 0))
    in_specs = [pl.BlockSpec(memory_space=pltpu.SMEM),
                tok,
                pl.BlockSpec((1, n, RET_QK_WIDTH), lambda b, i: (b, 0, 0)),
                pl.BlockSpec((1, n, RET_WIDTH), lambda b, i: (b, 0, 0)),
                tok2, tok,
                pl.BlockSpec((1, 3, D_MODEL), (lambda b, i: (b, 0, 0)) if mod_per_batch else (lambda b, i: (0, 0, 0))),
                _const_spec((1, RET_WIDTH)),
                _const_spec((RET_WIDTH, D_MODEL))]
    args = [lg, q, k.reshape(nb, n, RET_QK_WIDTH), v.reshape(nb, n, RET_WIDTH), g, x, mod, gn_w, w_out]
    st_block = (1, 2, RET_HEADS, RET_QK_DIM, RET_V_DIM)
    if s0 is not None:
        in_specs.append(pl.BlockSpec(st_block, lambda b, i: (b, 0, 0, 0, 0)))
        args.append(s0)
    out_specs = [tok]
    out_shape = [jax.ShapeDtypeStruct((t, D_MODEL), F32)]
    if emit_state:
        assert nq == 1
        out_specs.append(pl.BlockSpec(st_block, lambda b, i: (b, 0, 0, 0, 0)))
        out_shape.append(jax.ShapeDtypeStruct((nb,) + st_block[1:], F32))
    return pl.pallas_call(
        functools.partial(_ret_core_kernel, n=n, has_s0=s0 is not None, emit_state=emit_state),
        grid=(nb, nq),
        in_specs=in_specs,
        out_specs=out_specs,
        out_shape=out_shape,
        scratch_shapes=[pltpu.VMEM((tq, RET_WIDTH), BF16)],
        compiler_params=_params(2),
        name="ret_core",
    )(*args)


def _mlp_kernel(x_ref, mod_ref, nw_ref, wi_ref, lnw_ref, lnb_ref, ws_ref, bs_ref, wo_ref, y_ref, o_ref):
    x = x_ref[...]
    mod = mod_ref[0]
    h = _norm_mod(x, nw_ref[...], mod).astype(BF16)
    w = MLP_WIDTH
    v = _gelu_tanh(_dot(h, wi_ref[:, w:2 * w]))
    mu = jnp.mean(v, axis=-1, keepdims=True)
    vc = v - mu
    var = jnp.mean(vc * vc, axis=-1, keepdims=True)
    vn = (vc * lax.rsqrt(var + EPS) * lnw_ref[...] + lnb_ref[...]).astype(BF16)
    u = _gelu_tanh(_dot(h, wi_ref[:, 0:w]))
    ug = u * _silu(_dot(h, wi_ref[:, 2 * w:3 * w]))
    tm = x.shape[0]
    for c in range(tm // MLP_CHUNK):
        rs = slice(c * MLP_CHUNK, (c + 1) * MLP_CHUNK)
        for g in range(MLP_GROUPS):
            cs = slice(g * MLP_GROUP_DIM, (g + 1) * MLP_GROUP_DIM)
            sv = _dot(ws_ref[g], vn[rs, cs]) + bs_ref[:, cs]
            o_ref[rs, cs] = (ug[rs, cs] * sv).astype(BF16)
    y_ref[...] = x + mod[2:3] * _dot(o_ref[...], wo_ref[...])


def _mlp_layer(x, mod, nw, w_in, ln_w, ln_b, w_s, b_s_cols, w_out, blocks_per_batch):
    t = x.shape[0]
    tm = TOKEN_BLOCK
    tok = pl.BlockSpec((tm, D_MODEL), lambda i: (i, 0))
    return pl.pallas_call(
        _mlp_kernel,
        grid=(t // tm,),
        in_specs=[tok,
                  pl.BlockSpec((1, 3, D_MODEL), lambda i: (i // blocks_per_batch, 0, 0)),
                  _const_spec((1, D_MODEL)),
                  _const_spec((D_MODEL, 3 * MLP_WIDTH)),
                  _const_spec((1, MLP_WIDTH)),
                  _const_spec((1, MLP_WIDTH)),
                  _const_spec((MLP_GROUPS, MLP_CHUNK, MLP_CHUNK)),
                  _const_spec((MLP_CHUNK, MLP_WIDTH)),
                  _const_spec((MLP_WIDTH, D_MODEL))],
        out_specs=tok,
        out_shape=jax.ShapeDtypeStruct((t, D_MODEL), F32),
        scratch_shapes=[pltpu.VMEM((tm, MLP_WIDTH), BF16)],
        compiler_params=_params(1),
        name="mlp_layer",
    )(x, mod, nw, w_in, ln_w, ln_b, w_s, b_s_cols, w_out)


def kernel(x_prompt, x_sample, cache_na_k, cache_na_v, state_ret, c, c_ctx, norm_w, w_ada, b_ada,
           na_w_in, na_w_out, na_q_gain, na_k_gain, na_rpb,
           ret_w_in, ret_w_out, ret_decay_logit, ret_gn_w,
           mlp_w_in, mlp_ln_w, mlp_ln_b, mlp_w_s, mlp_b_s, mlp_w_out):
    batch, seq, _ = x_prompt.shape
    dec_batch, dec_seq, _ = x_sample.shape
    past = cache_na_k.shape[2]
    assert seq == TOKEN_BLOCK and dec_seq == 16 * GRID_W and dec_batch <= 7

    c_rows = jnp.zeros((8, D_MODEL), F32).at[:dec_batch].set(c).at[dec_batch].set(c_ctx)
    mods = _ada_all(c_rows, w_ada, b_ada).reshape(DEPTH, 8, 3, D_MODEL)

    head_tabs = _head_sum_tables()
    rope_tabs = _rope_tables(dec_seq)

    yp = x_prompt.reshape(batch * seq, D_MODEL)
    ys = x_sample.reshape(dec_batch * dec_seq, D_MODEL)
    bpb_s = dec_seq // TOKEN_BLOCK
    n_na = (DEPTH + 2) // N_MIXERS
    new_kv, new_s = None, []
    for i in range(DEPTH):
        kind, j = i % N_MIXERS, i // N_MIXERS
        mod_p = mods[i, dec_batch:dec_batch + 1]
        mod_s = mods[i, :dec_batch]
        nw = norm_w[i].reshape(1, D_MODEL)
        if kind == 0:
            w_in = na_w_in[j].astype(BF16)
            w_out = na_w_out[j].astype(BF16)
            qg = jnp.tile(na_q_gain[j], NA_HEADS).reshape(1, D_MODEL)
            kg = jnp.tile(na_k_gain[j], NA_HEADS).reshape(1, D_MODEL)
            q, k, v, g, *new_kv = _na_proj(yp, mod_p, nw, w_in, qg, kg, head_tabs, batch,
                                           cache_slot=(n_na, j), cache_prev=new_kv)
            yp = _na_prompt_attn(q, k, v, g, yp, mod_p, w_out)
            q, k, v, g = _na_proj(ys, mod_s, nw, w_in, qg, kg, head_tabs, bpb_s)
            kc = cache_na_k[:, j].transpose(0, 2, 3, 1).reshape(dec_batch, D_MODEL, past).astype(BF16)
            vc = cache_na_v[:, j].transpose(0, 2, 3, 1).reshape(dec_batch, D_MODEL, past).astype(BF16)
            pw = 2 * NA_HEAD_DIM
            vc = jnp.concatenate([vc.reshape(dec_batch, NA_HEADS // 2, pw, past),
                                  jnp.ones((dec_batch, NA_HEADS // 2, pw, past), BF16)],
                                 axis=2).reshape(dec_batch, 2 * D_MODEL, past)
            ys = _na_latent_attn(q, k.reshape(dec_batch, dec_seq, D_MODEL),
                                 v.reshape(dec_batch, dec_seq, 2 * D_MODEL), kc, vc,
                                 _na_bias_tables(na_rpb[j], dec_seq // GRID_W), g, ys, mod_s, w_out)
        elif kind == 1:
            w_in = ret_w_in[j].astype(BF16)
            w_out = ret_w_out[j].astype(BF16)
            lg = jax.nn.log_sigmoid(ret_decay_logit[j].astype(F32)).reshape(2 * RET_HEADS)
            gn = ret_gn_w[j].reshape(1, RET_WIDTH)
            q, k, v, g = _ret_proj(yp, mod_p, nw, w_in, batch, None)
            yp, st = _ret_core(lg, q, k, v, g, yp, mod_p, gn, w_out, seq, False, emit_state=True)
            new_s.append(st)
            q, k, v, g = _ret_proj(ys, mod_s, nw, w_in, bpb_s, rope_tabs)
            (ys,) = _ret_core(lg, q, k, v, g, ys, mod_s, gn, w_out, dec_seq, True,
                              s0=state_ret[:, j].astype(BF16))
        else:
            w_in = mlp_w_in[j].astype(BF16)
            w_out = mlp_w_out[j].astype(BF16)
            lnw = mlp_ln_w[j].reshape(1, MLP_WIDTH)
            lnb = mlp_ln_b[j].reshape(1, MLP_WIDTH)
            w_s = mlp_w_s[j].astype(BF16)
            b_cols = jnp.repeat(mlp_b_s[j].T, MLP_GROUP_DIM, axis=1)
            yp = _mlp_layer(yp, mod_p, nw, w_in, lnw, lnb, w_s, b_cols, w_out, batch)
            ys = _mlp_layer(ys, mod_s, nw, w_in, lnw, lnb, w_s, b_cols, w_out, bpb_s)
    return (yp.reshape(batch, seq, D_MODEL),
            ys.reshape(dec_batch, dec_seq, D_MODEL),
            *(t.reshape(batch, n_na, NA_HEADS, NA_HEAD_DIM, seq).transpose(0, 1, 4, 2, 3) for t in new_kv),
            jnp.stack(new_s, axis=1))
```

```python
import functools
import math

import numpy as np
import jax
import jax.numpy as jnp
from jax import lax
from jax.experimental import pallas as pl
from jax.experimental.pallas import tpu as pltpu

F32 = jnp.float32
BF16 = jnp.bfloat16

D_MODEL = 1024
DEPTH = 4
N_MIXERS = 3
GRID_W = 64
EPS = 1e-6
NEG_INF = -1e30
LOG2E = 1.4426950408889634

NA_HEADS = 16
NA_HEAD_DIM = 64
NA_PAIR = 2 * NA_HEAD_DIM
NA_KH = 8
NA_KW = 16
NA_Q_ROWS = 4
NA_WIN_ROWS = 12
NA_TQ = NA_Q_ROWS * GRID_W
NA_WIN = NA_WIN_ROWS * GRID_W

RET_HEADS = 4
RET_QK_DIM = 256
RET_V_DIM = 512
RET_QK_WIDTH = RET_HEADS * RET_QK_DIM
RET_WIDTH = RET_HEADS * RET_V_DIM
ROPE_BASE = 10000.0

MLP_WIDTH = 2048
MLP_GROUPS = 8
MLP_GROUP_DIM = 256
MLP_CHUNK = 128

TOKEN_BLOCK = 256
V7X_VMEM_LIMIT = 56 * 1024 * 1024

_NT = (((1,), (1,)), ((), ()))


def _dot(a, b):
    return jnp.dot(a, b, preferred_element_type=F32)


def _dot_nt(a, b):
    return lax.dot_general(a, b, _NT, preferred_element_type=F32)


def _silu(x):
    return x * jax.nn.sigmoid(x)


def _gelu_tanh(x):
    c = math.sqrt(2.0 / math.pi)
    return x * (0.5 * (1.0 + jnp.tanh(c * (x + 0.044715 * (x * x * x)))))


def _norm_mod(x, nw, mod):
    ms = jnp.mean(x * x, axis=-1, keepdims=True)
    y = x * lax.rsqrt(ms + EPS) * nw
    return y * (1.0 + mod[1:2]) + mod[0:1]


def _const_spec(shape):
    nd = len(shape)
    return pl.BlockSpec(shape, lambda *_: (0,) * nd, pipeline_mode=pl.Buffered(1))


def _layer_spec(shape, j):
    nd = len(shape)
    return pl.BlockSpec((None,) + tuple(shape), lambda *_: (j,) + (0,) * nd, pipeline_mode=pl.Buffered(1))


def _params(n_axes, vmem=V7X_VMEM_LIMIT):
    return pltpu.CompilerParams(dimension_semantics=("arbitrary",) * n_axes,
                                vmem_limit_bytes=vmem)


def _ada_kernel(c_ref, w_ref, b_ref, o_ref):
    a = _silu(c_ref[...]).astype(BF16)
    o_ref[0] = _dot(a, w_ref[0].astype(BF16)) + b_ref[0]


def _ada_all(c_rows, w_ada, b_ada):
    n_col = 3
    return pl.pallas_call(
        _ada_kernel,
        grid=(DEPTH, n_col),
        in_specs=[
            pl.BlockSpec((8, D_MODEL), lambda i, n: (0, 0)),
            pl.BlockSpec((1, D_MODEL, D_MODEL), lambda i, n: (i, 0, n)),
            pl.BlockSpec((1, 1, D_MODEL), lambda i, n: (i, 0, n)),
        ],
        out_specs=pl.BlockSpec((1, 8, D_MODEL), lambda i, n: (i, 0, n)),
        out_shape=jax.ShapeDtypeStruct((DEPTH, 8, 3 * D_MODEL), F32),
        compiler_params=_params(2, 32 * 1024 * 1024),
        name="adaln",
    )(c_rows, w_ada, b_ada.reshape(DEPTH, 1, 3 * D_MODEL))


def _head_sum_tables():
    lane_head = np.arange(D_MODEL) // NA_HEAD_DIM
    hsum = (lane_head[:, None] == np.arange(128)[None, :]).astype(np.float32)
    return jnp.asarray(hsum, BF16), jnp.asarray(np.concatenate([hsum.T, hsum.T], axis=0), BF16)


def _na_qkvg(x, mod, nw, w_ref, qg, kg, hsum_ref, hexp_ref):
    h = _norm_mod(x, nw, mod).astype(BF16)

    def head_norm(t, gain):
        ssq = _dot((t * t).astype(BF16), hsum_ref[...])
        r = lax.rsqrt(ssq * (1.0 / NA_HEAD_DIM) + EPS)
        hi = r.astype(BF16)
        lo = (r - hi.astype(F32)).astype(BF16)
        return t * _dot(jnp.concatenate([hi, lo], axis=1), hexp_ref[...]) * gain

    w = D_MODEL
    qn = head_norm(_dot(h, w_ref[:, 0:w]), qg)
    q = (qn * (LOG2E * NA_HEAD_DIM ** -0.5)).astype(BF16)
    kn = head_norm(_dot(h, w_ref[:, w:2 * w]), kg)
    v = _dot(h, w_ref[:, 2 * w:3 * w])
    g = _silu(_dot(h, w_ref[:, 3 * w:4 * w]))
    return q, kn, v, g


def _pair_masks():
    lane = lax.broadcasted_iota(jnp.int32, (1, NA_PAIR), 1)
    return lane < NA_HEAD_DIM


def _split_pair(q2, first):
    zero = jnp.zeros_like(q2)
    return jnp.where(first, q2, zero), jnp.where(first, zero, q2)


def _softmax_pv(s_parts, v_parts):
    m = functools.reduce(jnp.maximum, [jnp.max(s, axis=-1, keepdims=True) for s in s_parts])
    acc = None
    for s, (v, v_is_t) in zip(s_parts, v_parts):
        e = jnp.exp2(s - m).astype(BF16)
        o = _dot_nt(e, v) if v_is_t else _dot(e, v)
        acc = o if acc is None else acc + o
    return acc[:, :NA_PAIR] / acc[:, NA_PAIR:]


def _na_prompt_kernel(x_ref, mod_ref, nw_ref, w_ref, qg_ref, kg_ref, hsum_ref, hexp_ref, wo_ref, *rest,
                      n_alias):
    y_ref, kt_ref, vt_ref, og_ref = rest[n_alias:]
    x = x_ref[...]
    mod = mod_ref[0]
    q, kn, v, g = _na_qkvg(x, mod, nw_ref[...], w_ref, qg_ref[...], kg_ref[...], hsum_ref, hexp_ref)
    kt_ref[0, 0] = kn.T
    vt_ref[0, 0] = v.T
    kb, vb = kn.astype(BF16), v.astype(BF16)
    ones = jnp.ones((x.shape[0], NA_PAIR), BF16)
    first = _pair_masks()
    for p in range(NA_HEADS // 2):
        sl = slice(p * NA_PAIR, (p + 1) * NA_PAIR)
        k2 = kb[:, sl]
        v2 = jnp.concatenate([vb[:, sl], ones], axis=1)
        outs = [_softmax_pv([_dot_nt(qh, k2)], [(v2, False)]) for qh in _split_pair(q[:, sl], first)]
        og_ref[:, sl] = (jnp.where(first, outs[0], outs[1]) * g[:, sl]).astype(BF16)
    y_ref[...] = x + mod[2:3] * _dot(og_ref[...], wo_ref[...])


def _na_prompt_layer(x, mod, nw, w_in, w_out, j, q_gain, k_gain, head_tabs, n_layers, cache_prev):
    t = x.shape[0]
    tm = TOKEN_BLOCK
    tok = pl.BlockSpec((tm, D_MODEL), lambda i: (i, 0))
    in_specs = [tok,
                _const_spec((1, 3, D_MODEL)),
                _const_spec((1, D_MODEL)),
                _layer_spec((D_MODEL, 4 * D_MODEL), j),
                _const_spec((1, D_MODEL)),
                _const_spec((1, D_MODEL)),
                _const_spec((D_MODEL, 128)),
                _const_spec((256, D_MODEL)),
                _layer_spec((D_MODEL, D_MODEL), j)]
    args = [x, mod, nw, w_in, q_gain, k_gain, *head_tabs, w_out]
    cache = pl.BlockSpec((1, 1, D_MODEL, tm), lambda i: (i, j, 0, 0))
    cache_shape = jax.ShapeDtypeStruct((t // tm, n_layers, D_MODEL, tm), F32)
    aliases = {}
    if cache_prev is not None:
        aliases = {len(args): 1, len(args) + 1: 2}
        in_specs += [pl.BlockSpec(memory_space=pl.ANY)] * 2
        args += list(cache_prev)
    return pl.pallas_call(
        functools.partial(_na_prompt_kernel, n_alias=len(aliases)),
        grid=(t // tm,),
        in_specs=in_specs,
        out_specs=[tok, cache, cache],
        out_shape=[jax.ShapeDtypeStruct((t, D_MODEL), F32), cache_shape, cache_shape],
        input_output_aliases=aliases,
        scratch_shapes=[pltpu.VMEM((tm, D_MODEL), BF16)],
        compiler_params=_params(1),
        name="na_prompt",
    )(*args)


def _na_proj_kernel(x_ref, mod_ref, nw_ref, w_ref, qg_ref, kg_ref, hsum_ref, hexp_ref,
                    q_out, k_out, v_out, g_out):
    q, kn, v, g = _na_qkvg(x_ref[...], mod_ref[0], nw_ref[...], w_ref, qg_ref[...], kg_ref[...],
                           hsum_ref, hexp_ref)
    q_out[...] = q
    k_out[...] = kn.astype(BF16)
    g_out[...] = g.astype(BF16)
    vb = v.astype(BF16)
    ones = jnp.ones((vb.shape[0], NA_PAIR), BF16)
    for p in range(NA_HEADS // 2):
        v_out[:, 2 * p * NA_PAIR:(2 * p + 1) * NA_PAIR] = vb[:, p * NA_PAIR:(p + 1) * NA_PAIR]
        v_out[:, (2 * p + 1) * NA_PAIR:(2 * p + 2) * NA_PAIR] = ones


def _na_proj(x, mod, nw, w_in, j, q_gain, k_gain, head_tabs, blocks_per_batch):
    t = x.shape[0]
    tm = TOKEN_BLOCK
    tok = pl.BlockSpec((tm, D_MODEL), lambda i: (i, 0))
    tok_v = pl.BlockSpec((tm, 2 * D_MODEL), lambda i: (i, 0))
    return pl.pallas_call(
        _na_proj_kernel,
        grid=(t // tm,),
        in_specs=[tok,
                  pl.BlockSpec((1, 3, D_MODEL), lambda i: (i // blocks_per_batch, 0, 0)),
                  _const_spec((1, D_MODEL)),
                  _layer_spec((D_MODEL, 4 * D_MODEL), j),
                  _const_spec((1, D_MODEL)),
                  _const_spec((1, D_MODEL)),
                  _const_spec((D_MODEL, 128)),
                  _const_spec((256, D_MODEL))],
        out_specs=[tok, tok, tok_v, tok],
        out_shape=[jax.ShapeDtypeStruct((t, w), BF16) for w in (D_MODEL, D_MODEL, 2 * D_MODEL, D_MODEL)],
        compiler_params=_params(1),
        name="na_proj",
    )(x, mod, nw, w_in, q_gain, k_gain, *head_tabs)


def _na_latent_attn_kernel(q_ref, k_ref, v_ref, kc_ref, vc_ref, bias_ref, g_ref, x_ref, mod_ref,
                           wo_ref, y_ref, og_ref):
    i = pl.program_id(0)
    win0 = pl.multiple_of((i // 2) * NA_TQ, NA_TQ)
    first = _pair_masks()
    for p in range(NA_HEADS // 2):
        sl = slice(p * NA_PAIR, (p + 1) * NA_PAIR)
        sl2 = slice(2 * p * NA_PAIR, 2 * (p + 1) * NA_PAIR)
        kw = k_ref[0, pl.ds(win0, NA_WIN), sl]
        vw = v_ref[0, pl.ds(win0, NA_WIN), sl2]
        kc_t, vc_t = kc_ref[0, sl, :], vc_ref[0, sl2, :]
        outs = []
        for hh, qh in enumerate(_split_pair(q_ref[:, sl], first)):
            s_loc = _dot_nt(qh, kw) + bias_ref[2 * p + hh, 0].astype(F32)
            outs.append(_softmax_pv([s_loc, _dot(qh, kc_t)], [(vw, False), (vc_t, True)]))
        o2 = jnp.where(first, outs[0], outs[1])
        og_ref[:, sl] = (o2 * g_ref[:, sl].astype(F32)).astype(BF16)
    y_ref[...] = x_ref[...] + mod_ref[0][2:3] * _dot(og_ref[...], wo_ref[...])


def _na_latent_attn(q, k, v, kc, vc, bias, g, x, mod, w_out, j):
    nb, n = k.shape[0], k.shape[1]
    nq = n // NA_TQ
    tok = pl.BlockSpec((NA_TQ, D_MODEL), lambda i, b: (b * nq + i, 0))
    full = pl.BlockSpec((1, n, D_MODEL), lambda i, b: (b, 0, 0))
    full2 = pl.BlockSpec((1, n, 2 * D_MODEL), lambda i, b: (b, 0, 0))
    ctx = pl.BlockSpec((1, D_MODEL, kc.shape[2]), lambda i, b: (b, 0, 0))
    ctx2 = pl.BlockSpec((1, 2 * D_MODEL, kc.shape[2]), lambda i, b: (b, 0, 0))
    return pl.pallas_call(
        _na_latent_attn_kernel,
        grid=(nq, nb),
        in_specs=[tok, full, full2, ctx, ctx2,
                  pl.BlockSpec((NA_HEADS, 1, NA_TQ, NA_WIN), lambda i, b: (0, i, 0, 0)),
                  tok, tok,
                  pl.BlockSpec((1, 3, D_MODEL), lambda i, b: (b, 0, 0)),
                  _layer_spec((D_MODEL, D_MODEL), j)],
        out_specs=tok,
        out_shape=jax.ShapeDtypeStruct((nb * n, D_MODEL), F32),
        scratch_shapes=[pltpu.VMEM((NA_TQ, D_MODEL), BF16)],
        compiler_params=_params(2),
        name="na_latent_attn",
    )(q, k, v, kc, vc, bias, g, x, mod, w_out)


def _na_window_plan(rows):
    n_blk = rows // NA_Q_ROWS
    kh = min(NA_KH, rows)
    n_chunk = NA_WIN_ROWS // 2
    a0 = np.zeros((n_blk, NA_Q_ROWS, n_chunk), np.int64)
    row_mask = np.zeros((n_blk, NA_TQ, NA_WIN), np.float32)
    for i in range(n_blk):
        win_row0 = (i // 2) * NA_Q_ROWS
        for ql in range(NA_Q_ROWS):
            qr = i * NA_Q_ROWS + ql
            rstart = min(max(qr - kh // 2, 0), rows - kh)
            for kl in range(NA_WIN_ROWS):
                kr = win_row0 + kl
                if not rstart <= kr < rstart + kh:
                    row_mask[i, ql * GRID_W:(ql + 1) * GRID_W, kl * GRID_W:(kl + 1) * GRID_W] = NEG_INF
            for m in range(n_chunk):
                a0[i, ql, m] = win_row0 + 2 * m - qr + (NA_KH - 1)
    return a0, row_mask


def _na_bias_kernel(t2_ref, rm_ref, o_ref, *, idx):
    n_blk, n_ql, n_chunk = idx.shape
    for i in range(n_blk):
        for ql in range(n_ql):
            rs = slice(ql * GRID_W, (ql + 1) * GRID_W)
            for m in range(n_chunk):
                cs = slice(m * 128, (m + 1) * 128)
                o_ref[0, i, rs, cs] = t2_ref[0, int(idx[i, ql, m])] + rm_ref[i, rs, cs]


def _na_bias_tables(rpb, rows):
    a0, row_mask = _na_window_plan(rows)
    n_blk = a0.shape[0]
    cq = np.arange(GRID_W)
    cstart = np.clip(cq - NA_KW // 2, 0, GRID_W - NA_KW)
    col_ok = (cq[None, :] >= cstart[:, None]) & (cq[None, :] < cstart[:, None] + NA_KW)
    coff = np.clip(cq[None, :] - cq[:, None], -(NA_KW - 1), NA_KW - 1) + (NA_KW - 1)
    onehot = (coff[None] == np.arange(2 * NA_KW - 1)[:, None, None]) & col_ok[None]
    t = jnp.einsum("hab,bqk->haqk", rpb, jnp.asarray(onehot, F32), precision=lax.Precision.HIGHEST)
    t = jnp.where(col_ok[None, None], t * LOG2E, NEG_INF)
    lo, hi = int(a0.min()), int(a0.max()) + 1
    n_off = 2 * NA_KH - 1
    tm = jnp.pad(t, ((0, 0), (max(-lo, 0), max(hi - n_off + 1, 0)), (0, 0), (0, 0)), constant_values=NEG_INF)
    n_pair = hi - lo
    t2 = jnp.concatenate([tm[:, 0:n_pair], tm[:, 1:n_pair + 1]], axis=-1).astype(BF16)
    return pl.pallas_call(
        functools.partial(_na_bias_kernel, idx=a0 - lo),
        grid=(NA_HEADS,),
        in_specs=[pl.BlockSpec((1, n_pair, GRID_W, 128), lambda h: (h, 0, 0, 0)),
                  _const_spec((n_blk, NA_TQ, NA_WIN))],
        out_specs=pl.BlockSpec((1, n_blk, NA_TQ, NA_WIN), lambda h: (h, 0, 0, 0)),
        out_shape=jax.ShapeDtypeStruct((NA_HEADS, n_blk, NA_TQ, NA_WIN), BF16),
        compiler_params=_params(1, 32 * 1024 * 1024),
        name="na_bias",
    )(t2, jnp.asarray(row_mask, BF16))


def _ret_qkvg(x, mod, nw, w_ref, rope_refs):
    h = _norm_mod(x, nw, mod).astype(BF16)
    qw, vw = RET_QK_WIDTH, RET_WIDTH

    def rot(t):
        if rope_refs is None:
            return t
        cos, sin = rope_refs[0][...], rope_refs[1][...]
        parts = []
        for c in range(qw // 128):
            sl = slice(c * 128, (c + 1) * 128)
            tc = t[:, sl]
            parts.append(tc * cos[:, sl] + pltpu.roll(tc, 64, axis=1) * sin[:, sl])
        return jnp.concatenate(parts, axis=1)

    q = rot(_dot(h, w_ref[:, 0:qw])).astype(BF16)
    k = (rot(_dot(h, w_ref[:, qw:2 * qw])) * (RET_QK_DIM ** -0.5)).astype(BF16)
    v = _dot(h, w_ref[:, 2 * qw:2 * qw + vw]).astype(BF16)
    g = _silu(_dot(h, w_ref[:, 2 * qw + vw:2 * qw + 2 * vw]))
    return q, k, v, g


def _ret_decay_init(lg_ref, dec_ref, row0):
    _, tq, n = dec_ref.shape
    qi = (row0 + lax.broadcasted_iota(jnp.int32, (tq, n), 0)).astype(F32)
    diff = qi - lax.broadcasted_iota(jnp.int32, (tq, n), 1).astype(F32)
    for h in range(RET_HEADS):
        lgf, lgb = lg_ref[h], lg_ref[RET_HEADS + h]
        dec = jnp.exp(jnp.where(diff >= 0.0, lgf * diff, -lgb * diff))
        dec_ref[h] = jnp.where(diff == 0.0, 2.0, dec)


def _ret_mix(lg_ref, dec_ref, q, k, v, g, gn_ref, yg_ref, row0, s0_ref, st_ref):
    _, tq, n = dec_ref.shape
    for h in range(RET_HEADS):
        lgf, lgb = lg_ref[h], lg_ref[RET_HEADS + h]
        qh = q[:, h * RET_QK_DIM:(h + 1) * RET_QK_DIM]
        kh = k[:, h * RET_QK_DIM:(h + 1) * RET_QK_DIM]
        vh = v[:, h * RET_V_DIM:(h + 1) * RET_V_DIM]
        inner = (_dot_nt(qh, kh) * dec_ref[h]).astype(BF16)
        o = _dot(inner, vh)
        if s0_ref is not None:
            pos = (row0 + lax.broadcasted_iota(jnp.int32, (tq, 1), 0)).astype(F32)
            o = o + _dot(qh, s0_ref[0, 0, h]) * jnp.exp(lgf * (pos + 1.0))
            o = o + _dot(qh, s0_ref[0, 1, h]) * jnp.exp(lgb * (float(n) - pos))
        if st_ref is not None:
            kpos = lax.broadcasted_iota(jnp.int32, (n, RET_QK_DIM), 0).astype(F32)
            kf = kh.astype(F32)
            kdf = (kf * jnp.exp(lgf * (float(n - 1) - kpos))).T.astype(BF16)
            kdb = (kf * jnp.exp(lgb * kpos)).T.astype(BF16)
            st_ref[0, 0, h] = _dot(kdf, vh)
            st_ref[0, 1, h] = _dot(kdb, vh)
        mu = jnp.mean(o, axis=-1, keepdims=True)
        oc = o - mu
        var = jnp.mean(oc * oc, axis=-1, keepdims=True)
        vs = slice(h * RET_V_DIM, (h + 1) * RET_V_DIM)
        yn = oc * lax.rsqrt(var + EPS) * gn_ref[:, vs]
        yg_ref[:, vs] = (yn * g[:, vs].astype(F32)).astype(BF16)


def _ret_prompt_kernel(lg_ref, x_ref, mod_ref, nw_ref, w_ref, gn_ref, wo_ref, y_ref, st_ref, dec_ref, yg_ref):
    @pl.when(pl.program_id(0) == 0)
    def _():
        _ret_decay_init(lg_ref, dec_ref, 0)

    x = x_ref[...]
    mod = mod_ref[0]
    q, k, v, g = _ret_qkvg(x, mod, nw_ref[...], w_ref, None)
    _ret_mix(lg_ref, dec_ref, q, k, v, g, gn_ref, yg_ref, 0, None, st_ref)
    y_ref[...] = x + mod[2:3] * _dot(yg_ref[...], wo_ref[...])


def _ret_prompt_layer(lg, x, mod, nw, w_in, w_out, j, gn_w):
    t = x.shape[0]
    n = TOKEN_BLOCK
    tok = pl.BlockSpec((n, D_MODEL), lambda b: (b, 0))
    st_block = (1, 2, RET_HEADS, RET_QK_DIM, RET_V_DIM)
    return pl.pallas_call(
        _ret_prompt_kernel,
        grid=(t // n,),
        in_specs=[pl.BlockSpec(memory_space=pltpu.SMEM),
                  tok,
                  _const_spec((1, 3, D_MODEL)),
                  _const_spec((1, D_MODEL)),
                  _layer_spec((D_MODEL, 2 * RET_QK_WIDTH + 2 * RET_WIDTH), j),
                  _const_spec((1, RET_WIDTH)),
                  _layer_spec((RET_WIDTH, D_MODEL), j)],
        out_specs=[tok, pl.BlockSpec(st_block, lambda b: (b, 0, 0, 0, 0))],
        out_shape=[jax.ShapeDtypeStruct((t, D_MODEL), F32),
                   jax.ShapeDtypeStruct((t // n,) + st_block[1:], F32)],
        scratch_shapes=[pltpu.VMEM((RET_HEADS, n, n), F32), pltpu.VMEM((n, RET_WIDTH), BF16)],
        compiler_params=_params(1),
        name="ret_prompt",
    )(lg, x, mod, nw, w_in, gn_w, w_out)


def _ret_proj_kernel(x_ref, mod_ref, nw_ref, w_ref, cos_ref, sin_ref, q_out, k_out, v_out, g_out):
    q, k, v, g = _ret_qkvg(x_ref[...], mod_ref[0], nw_ref[...], w_ref, (cos_ref, sin_ref))
    q_out[...] = q
    k_out[...] = k
    v_out[...] = v
    g_out[...] = g.astype(BF16)


def _ret_proj(x, mod, nw, w_in, j, blocks_per_batch, rope_tables):
    t = x.shape[0]
    tm = TOKEN_BLOCK
    tok = pl.BlockSpec((tm, D_MODEL), lambda i: (i, 0))
    tok2 = pl.BlockSpec((tm, RET_WIDTH), lambda i: (i, 0))
    pos = pl.BlockSpec((tm, RET_QK_WIDTH), lambda i: (i % blocks_per_batch, 0))
    return pl.pallas_call(
        _ret_proj_kernel,
        grid=(t // tm,),
        in_specs=[tok,
                  pl.BlockSpec((1, 3, D_MODEL), lambda i: (i // blocks_per_batch, 0, 0)),
                  _const_spec((1, D_MODEL)),
                  _layer_spec((D_MODEL, 2 * RET_QK_WIDTH + 2 * RET_WIDTH), j),
                  pos, pos],
        out_specs=[tok, tok, tok2, tok2],
        out_shape=[jax.ShapeDtypeStruct((t, RET_QK_WIDTH), BF16)] * 2
        + [jax.ShapeDtypeStruct((t, RET_WIDTH), BF16)] * 2,
        compiler_params=_params(1),
        name="ret_proj",
    )(x, mod, nw, w_in, *rope_tables)


def _rope_tables(n):
    half = RET_QK_DIM // 2
    t = np.arange(n)
    inv = ROPE_BASE ** (-jnp.arange(0, half, 2, dtype=F32) / half)
    tabs = []
    for pos in ((t // GRID_W).astype(np.float32), (t % GRID_W).astype(np.float32)):
        ang = jnp.asarray(pos)[:, None] * inv[None, :]
        tabs.append((jnp.cos(ang), jnp.sin(ang)))
    cos = jnp.concatenate([tabs[0][0], tabs[0][0], tabs[1][0], tabs[1][0]], axis=1)
    sin = jnp.concatenate([-tabs[0][1], tabs[0][1], -tabs[1][1], tabs[1][1]], axis=1)
    return jnp.tile(cos, (1, RET_HEADS)), jnp.tile(sin, (1, RET_HEADS))


def _ret_core_kernel(lg_ref, q_ref, k_ref, v_ref, g_ref, x_ref, mod_ref, gn_ref, wo_ref, s0_ref,
                     y_ref, dec_ref, yg_ref):
    row0 = pl.program_id(0) * q_ref.shape[0]

    @pl.when(pl.program_id(1) == 0)
    def _():
        _ret_decay_init(lg_ref, dec_ref, row0)

    _ret_mix(lg_ref, dec_ref, q_ref, k_ref.at[0], v_ref.at[0], g_ref, gn_ref, yg_ref, row0, s0_ref, None)
    y_ref[...] = x_ref[...] + mod_ref[0][2:3] * _dot(yg_ref[...], wo_ref[...])


def _ret_core(lg, q, k, v, g, x, mod, gn_w, w_out, j, n, s0):
    t = x.shape[0]
    nb = t // n
    tq = TOKEN_BLOCK
    nq = n // tq
    tok = pl.BlockSpec((tq, D_MODEL), lambda i, b: (b * nq + i, 0))
    tok2 = pl.BlockSpec((tq, RET_WIDTH), lambda i, b: (b * nq + i, 0))
    st_block = (1, 2, RET_HEADS, RET_QK_DIM, RET_V_DIM)
    return pl.pallas_call(
        _ret_core_kernel,
        grid=(nq, nb),
        in_specs=[pl.BlockSpec(memory_space=pltpu.SMEM),
                  tok,
                  pl.BlockSpec((1, n, RET_QK_WIDTH), lambda i, b: (b, 0, 0)),
                  pl.BlockSpec((1, n, RET_WIDTH), lambda i, b: (b, 0, 0)),
                  tok2, tok,
                  pl.BlockSpec((1, 3, D_MODEL), lambda i, b: (b, 0, 0)),
                  _const_spec((1, RET_WIDTH)),
                  _layer_spec((RET_WIDTH, D_MODEL), j),
                  pl.BlockSpec(st_block, lambda i, b: (b, 0, 0, 0, 0))],
        out_specs=tok,
        out_shape=jax.ShapeDtypeStruct((t, D_MODEL), F32),
        scratch_shapes=[pltpu.VMEM((RET_HEADS, tq, n), F32), pltpu.VMEM((tq, RET_WIDTH), BF16)],
        compiler_params=_params(2),
        name="ret_core",
    )(lg, q, k.reshape(nb, n, RET_QK_WIDTH), v.reshape(nb, n, RET_WIDTH), g, x, mod, gn_w, w_out, s0)


def _mlp_kernel(x_ref, mod_ref, nw_ref, wi_ref, lnw_ref, lnb_ref, ws_ref, bs_ref, wo_ref, y_ref, o_ref):
    x = x_ref[...]
    mod = mod_ref[0]
    h = _norm_mod(x, nw_ref[...], mod).astype(BF16)
    w = MLP_WIDTH
    v = _gelu_tanh(_dot(h, wi_ref[:, w:2 * w]))
    mu = jnp.mean(v, axis=-1, keepdims=True)
    vc = v - mu
    var = jnp.mean(vc * vc, axis=-1, keepdims=True)
    vn = (vc * lax.rsqrt(var + EPS) * lnw_ref[...] + lnb_ref[...]).astype(BF16)
    u = _gelu_tanh(_dot(h, wi_ref[:, 0:w]))
    ug = u * _silu(_dot(h, wi_ref[:, 2 * w:3 * w]))
    tm = x.shape[0]
    for c in range(tm // MLP_CHUNK):
        rs = slice(c * MLP_CHUNK, (c + 1) * MLP_CHUNK)
        for g in range(MLP_GROUPS):
            cs = slice(g * MLP_GROUP_DIM, (g + 1) * MLP_GROUP_DIM)
            sv = _dot(ws_ref[g], vn[rs, cs]) + bs_ref[:, cs]
            o_ref[rs, cs] = (ug[rs, cs] * sv).astype(BF16)
    y_ref[...] = x + mod[2:3] * _dot(o_ref[...], wo_ref[...])


def _mlp_layer(x, mod, nw, w_in, ln_w, ln_b, w_s, b_s_cols, w_out, j, blocks_per_batch):
    t = x.shape[0]
    tm = TOKEN_BLOCK
    tok = pl.BlockSpec((tm, D_MODEL), lambda i: (i, 0))
    return pl.pallas_call(
        _mlp_kernel,
        grid=(t // tm,),
        in_specs=[tok,
                  pl.BlockSpec((1, 3, D_MODEL), lambda i: (i // blocks_per_batch, 0, 0)),
                  _const_spec((1, D_MODEL)),
                  _layer_spec((D_MODEL, 3 * MLP_WIDTH), j),
                  _const_spec((1, MLP_WIDTH)),
                  _const_spec((1, MLP_WIDTH)),
                  _layer_spec((MLP_GROUPS, MLP_CHUNK, MLP_CHUNK), j),
                  _const_spec((MLP_CHUNK, MLP_WIDTH)),
                  _layer_spec((MLP_WIDTH, D_MODEL), j)],
        out_specs=tok,
        out_shape=jax.ShapeDtypeStruct((t, D_MODEL), F32),
        scratch_shapes=[pltpu.VMEM((tm, MLP_WIDTH), BF16)],
        compiler_params=_params(1),
        name="mlp_layer",
    )(x, mod, nw, w_in, ln_w, ln_b, w_s, b_s_cols, w_out)


def kernel(x_prompt, x_sample, cache_na_k, cache_na_v, state_ret, c, c_ctx, norm_w, w_ada, b_ada,
           na_w_in, na_w_out, na_q_gain, na_k_gain, na_rpb,
           ret_w_in, ret_w_out, ret_decay_logit, ret_gn_w,
           mlp_w_in, mlp_ln_w, mlp_ln_b, mlp_w_s, mlp_b_s, mlp_w_out):
    batch, seq, _ = x_prompt.shape
    dec_batch, dec_seq, _ = x_sample.shape
    past = cache_na_k.shape[2]
    assert seq == TOKEN_BLOCK and dec_seq == 16 * GRID_W and dec_batch <= 7

    c_rows = jnp.zeros((8, D_MODEL), F32).at[:dec_batch].set(c).at[dec_batch].set(c_ctx)
    mods = _ada_all(c_rows, w_ada, b_ada).reshape(DEPTH, 8, 3, D_MODEL)

    head_tabs = _head_sum_tables()
    rope_tabs = _rope_tables(dec_seq)
    na_w_in, na_w_out = na_w_in.astype(BF16), na_w_out.astype(BF16)
    ret_w_in, ret_w_out = ret_w_in.astype(BF16), ret_w_out.astype(BF16)
    mlp_w_in, mlp_w_out, mlp_w_s = mlp_w_in.astype(BF16), mlp_w_out.astype(BF16), mlp_w_s.astype(BF16)

    yp = x_prompt.reshape(batch * seq, D_MODEL)
    ys = x_sample.reshape(dec_batch * dec_seq, D_MODEL)
    bpb_s = dec_seq // TOKEN_BLOCK
    n_na = (DEPTH + 2) // N_MIXERS
    new_kv, new_s = None, []
    for i in range(DEPTH):
        kind, j = i % N_MIXERS, i // N_MIXERS
        mod_p = mods[i, dec_batch:dec_batch + 1]
        mod_s = mods[i, :dec_batch]
        nw = norm_w[i].reshape(1, D_MODEL)
        if kind == 0:
            qg = jnp.tile(na_q_gain[j], NA_HEADS).reshape(1, D_MODEL)
            kg = jnp.tile(na_k_gain[j], NA_HEADS).reshape(1, D_MODEL)
            yp, *new_kv = _na_prompt_layer(yp, mod_p, nw, na_w_in, na_w_out, j, qg, kg, head_tabs, n_na, new_kv)
            q, k, v, g = _na_proj(ys, mod_s, nw, na_w_in, j, qg, kg, head_tabs, bpb_s)
            kc = cache_na_k[:, j].transpose(0, 2, 3, 1).reshape(dec_batch, D_MODEL, past).astype(BF16)
            vc = cache_na_v[:, j].transpose(0, 2, 3, 1).astype(BF16)
            vc = jnp.concatenate([vc.reshape(dec_batch, NA_HEADS // 2, NA_PAIR, past),
                                  jnp.ones((dec_batch, NA_HEADS // 2, NA_PAIR, past), BF16)],
                                 axis=2).reshape(dec_batch, 2 * D_MODEL, past)
            ys = _na_latent_attn(q, k.reshape(dec_batch, dec_seq, D_MODEL),
                                 v.reshape(dec_batch, dec_seq, 2 * D_MODEL), kc, vc,
                                 _na_bias_tables(na_rpb[j], dec_seq // GRID_W), g, ys, mod_s, na_w_out, j)
        elif kind == 1:
            lg = jax.nn.log_sigmoid(ret_decay_logit[j].astype(F32)).reshape(2 * RET_HEADS)
            gn = ret_gn_w[j].reshape(1, RET_WIDTH)
            yp, st = _ret_prompt_layer(lg, yp, mod_p, nw, ret_w_in, ret_w_out, j, gn)
            new_s.append(st)
            q, k, v, g = _ret_proj(ys, mod_s, nw, ret_w_in, j, bpb_s, rope_tabs)
            ys = _ret_core(lg, q, k, v, g, ys, mod_s, gn, ret_w_out, j, dec_seq, state_ret[:, j].astype(BF16))
        else:
            lnw = mlp_ln_w[j].reshape(1, MLP_WIDTH)
            lnb = mlp_ln_b[j].reshape(1, MLP_WIDTH)
            b_cols = jnp.repeat(mlp_b_s[j].T, MLP_GROUP_DIM, axis=1)
            yp = _mlp_layer(yp, mod_p, nw, mlp_w_in, lnw, lnb, mlp_w_s, b_cols, mlp_w_out, j, batch)
            ys = _mlp_layer(ys, mod_s, nw, mlp_w_in, lnw, lnb, mlp_w_s, b_cols, mlp_w_out, j, bpb_s)
    return (yp.reshape(batch, seq, D_MODEL),
            ys.reshape(dec_batch, dec_seq, D_MODEL),
            *(t.reshape(batch, n_na, NA_HEADS, NA_HEAD_DIM, seq).transpose(0, 1, 4, 2, 3) for t in new_kv),
            jnp.stack(new_s, axis=1))
```

```python
import functools
import math

import numpy as np
import jax
import jax.numpy as jnp
from jax import lax
from jax.experimental import pallas as pl
from jax.experimental.pallas import tpu as pltpu

F32 = jnp.float32
BF16 = jnp.bfloat16

D_MODEL = 1024
DEPTH = 4
N_MIXERS = 3
GRID_W = 64
EPS = 1e-6
NEG_INF = -1e30
LOG2E = 1.4426950408889634

NA_HEADS = 16
NA_HEAD_DIM = 64
NA_PAIR = 2 * NA_HEAD_DIM
NA_KH = 8
NA_KW = 16
NA_Q_ROWS = 4
NA_WIN_ROWS = 12
NA_TQ = NA_Q_ROWS * GRID_W
NA_WIN = NA_WIN_ROWS * GRID_W

RET_HEADS = 4
RET_QK_DIM = 256
RET_V_DIM = 512
RET_QK_WIDTH = RET_HEADS * RET_QK_DIM
RET_WIDTH = RET_HEADS * RET_V_DIM
ROPE_BASE = 10000.0

MLP_WIDTH = 2048
MLP_GROUPS = 8
MLP_GROUP_DIM = 256
MLP_CHUNK = 128

TOKEN_BLOCK = 256
V7X_VMEM_LIMIT = 56 * 1024 * 1024

_NT = (((1,), (1,)), ((), ()))


def _dot(a, b):
    return jnp.dot(a, b, preferred_element_type=F32)


def _dot_nt(a, b):
    return lax.dot_general(a, b, _NT, preferred_element_type=F32)


def _silu(x):
    return x * jax.nn.sigmoid(x)


def _gelu_tanh(x):
    c = math.sqrt(2.0 / math.pi)
    return x * (0.5 * (1.0 + jnp.tanh(c * (x + 0.044715 * (x * x * x)))))


def _norm_mod(x, nw, mod):
    ms = jnp.mean(x * x, axis=-1, keepdims=True)
    y = x * lax.rsqrt(ms + EPS) * nw
    return y * (1.0 + mod[1:2]) + mod[0:1]


def _const_spec(shape):
    nd = len(shape)
    return pl.BlockSpec(shape, lambda *_: (0,) * nd, pipeline_mode=pl.Buffered(1))


def _layer_spec(shape, j):
    nd = len(shape)
    return pl.BlockSpec((None,) + tuple(shape), lambda *_: (j,) + (0,) * nd, pipeline_mode=pl.Buffered(1))


def _params(n_axes, vmem=V7X_VMEM_LIMIT):
    return pltpu.CompilerParams(dimension_semantics=("arbitrary",) * n_axes,
                                vmem_limit_bytes=vmem)


def _ada_kernel(c_ref, w_ref, b_ref, o_ref):
    a = _silu(c_ref[...]).astype(BF16)
    o_ref[0] = _dot(a, w_ref[0].astype(BF16)) + b_ref[0]


def _ada_all(c_rows, w_ada, b_ada):
    n_col = 3
    return pl.pallas_call(
        _ada_kernel,
        grid=(DEPTH, n_col),
        in_specs=[
            pl.BlockSpec((8, D_MODEL), lambda i, n: (0, 0)),
            pl.BlockSpec((1, D_MODEL, D_MODEL), lambda i, n: (i, 0, n)),
            pl.BlockSpec((1, 1, D_MODEL), lambda i, n: (i, 0, n)),
        ],
        out_specs=pl.BlockSpec((1, 8, D_MODEL), lambda i, n: (i, 0, n)),
        out_shape=jax.ShapeDtypeStruct((DEPTH, 8, 3 * D_MODEL), F32),
        compiler_params=_params(2, 32 * 1024 * 1024),
        name="adaln",
    )(c_rows, w_ada, b_ada.reshape(DEPTH, 1, 3 * D_MODEL))


def _head_rms_t(t_t):
    slabs = []
    for h in range(NA_HEADS):
        blk = t_t[h * NA_HEAD_DIM:(h + 1) * NA_HEAD_DIM, :]
        ms = jnp.mean(blk * blk, axis=0, keepdims=True)
        slabs.append(blk * lax.rsqrt(ms + EPS))
    return jnp.concatenate(slabs, axis=0)


def _na_qkvg(x, mod, nw, w_ref, qg, kg_cols):
    h = _norm_mod(x, nw, mod).astype(BF16)
    w = D_MODEL
    qn = _head_rms_t(_dot(h, w_ref[:, 0:w]).T).T * qg
    q = (qn * (LOG2E * NA_HEAD_DIM ** -0.5)).astype(BF16)
    kn_t = _head_rms_t(_dot(h, w_ref[:, w:2 * w]).T)
    kn_t = kn_t * jnp.concatenate([kg_cols] * (kn_t.shape[1] // kg_cols.shape[1]), axis=1)
    v = _dot(h, w_ref[:, 2 * w:3 * w])
    g = _silu(_dot(h, w_ref[:, 3 * w:4 * w]))
    return q, kn_t, v, g


def _pair_masks():
    lane = lax.broadcasted_iota(jnp.int32, (1, NA_PAIR), 1)
    return lane < NA_HEAD_DIM


def _split_pair(q2, first):
    zero = jnp.zeros_like(q2)
    return jnp.where(first, q2, zero), jnp.where(first, zero, q2)


def _softmax_pv(s_parts, v_parts):
    m = functools.reduce(jnp.maximum, [jnp.max(s, axis=-1, keepdims=True) for s in s_parts])
    acc = None
    for s, (v, v_is_t) in zip(s_parts, v_parts):
        e = jnp.exp2(s - m).astype(BF16)
        o = _dot_nt(e, v) if v_is_t else _dot(e, v)
        acc = o if acc is None else acc + o
    return acc[:, :NA_PAIR] / acc[:, NA_PAIR:]


def _na_prompt_kernel(x_ref, mod_ref, nw_ref, w_ref, qg_ref, kg_ref, wo_ref, *rest, n_alias):
    y_ref, kt_ref, vt_ref, og_ref = rest[n_alias:]
    x = x_ref[...]
    mod = mod_ref[0]
    q, kn_t, v, g = _na_qkvg(x, mod, nw_ref[...], w_ref, qg_ref[...], kg_ref[...])
    kt_ref[0, 0] = kn_t
    vt_ref[0, 0] = v.T
    kb_t, vb = kn_t.astype(BF16), v.astype(BF16)
    ones = jnp.ones((x.shape[0], NA_PAIR), BF16)
    first = _pair_masks()
    for p in range(NA_HEADS // 2):
        sl = slice(p * NA_PAIR, (p + 1) * NA_PAIR)
        k2_t = kb_t[sl, :]
        v2 = jnp.concatenate([vb[:, sl], ones], axis=1)
        outs = [_softmax_pv([_dot(qh, k2_t)], [(v2, False)]) for qh in _split_pair(q[:, sl], first)]
        og_ref[:, sl] = (jnp.where(first, outs[0], outs[1]) * g[:, sl]).astype(BF16)
    y_ref[...] = x + mod[2:3] * _dot(og_ref[...], wo_ref[...])


def _na_prompt_layer(x, mod, nw, w_in, w_out, j, q_gain, k_gain_cols, n_layers, cache_prev):
    t = x.shape[0]
    tm = TOKEN_BLOCK
    tok = pl.BlockSpec((tm, D_MODEL), lambda i: (i, 0))
    in_specs = [tok,
                _const_spec((1, 3, D_MODEL)),
                _const_spec((1, D_MODEL)),
                _layer_spec((D_MODEL, 4 * D_MODEL), j),
                _const_spec((1, D_MODEL)),
                _const_spec((D_MODEL, 128)),
                _layer_spec((D_MODEL, D_MODEL), j)]
    args = [x, mod, nw, w_in, q_gain, k_gain_cols, w_out]
    cache = pl.BlockSpec((1, 1, D_MODEL, tm), lambda i: (i, j, 0, 0))
    cache_shape = jax.ShapeDtypeStruct((t // tm, n_layers, D_MODEL, tm), F32)
    aliases = {}
    if cache_prev is not None:
        aliases = {len(args): 1, len(args) + 1: 2}
        in_specs += [pl.BlockSpec(memory_space=pl.ANY)] * 2
        args += list(cache_prev)
    return pl.pallas_call(
        functools.partial(_na_prompt_kernel, n_alias=len(aliases)),
        grid=(t // tm,),
        in_specs=in_specs,
        out_specs=[tok, cache, cache],
        out_shape=[jax.ShapeDtypeStruct((t, D_MODEL), F32), cache_shape, cache_shape],
        input_output_aliases=aliases,
        scratch_shapes=[pltpu.VMEM((tm, D_MODEL), BF16)],
        compiler_params=_params(1),
        name="na_prompt",
    )(*args)


def _na_proj_kernel(x_ref, mod_ref, nw_ref, w_ref, qg_ref, kg_ref, q_out, kt_out, v_out, g_out):
    q, kn_t, v, g = _na_qkvg(x_ref[...], mod_ref[0], nw_ref[...], w_ref, qg_ref[...], kg_ref[...])
    q_out[...] = q
    kt_out[0, 0] = kn_t.astype(BF16)
    g_out[...] = g.astype(BF16)
    vb = v.astype(BF16)
    ones = jnp.ones((vb.shape[0], NA_PAIR), BF16)
    for p in range(NA_HEADS // 2):
        v_out[:, 2 * p * NA_PAIR:(2 * p + 1) * NA_PAIR] = vb[:, p * NA_PAIR:(p + 1) * NA_PAIR]
        v_out[:, (2 * p + 1) * NA_PAIR:(2 * p + 2) * NA_PAIR] = ones


def _na_proj(x, mod, nw, w_in, j, q_gain, k_gain_cols, blocks_per_batch):
    t = x.shape[0]
    tm = TOKEN_BLOCK
    bpb = blocks_per_batch
    tok = pl.BlockSpec((tm, D_MODEL), lambda i: (i, 0))
    tok_v = pl.BlockSpec((tm, 2 * D_MODEL), lambda i: (i, 0))
    return pl.pallas_call(
        _na_proj_kernel,
        grid=(t // tm,),
        in_specs=[tok,
                  pl.BlockSpec((1, 3, D_MODEL), lambda i: (i // bpb, 0, 0)),
                  _const_spec((1, D_MODEL)),
                  _layer_spec((D_MODEL, 4 * D_MODEL), j),
                  _const_spec((1, D_MODEL)),
                  _const_spec((D_MODEL, 128))],
        out_specs=[tok, pl.BlockSpec((1, 1, D_MODEL, tm), lambda i: (i // bpb, i % bpb, 0, 0)), tok_v, tok],
        out_shape=[jax.ShapeDtypeStruct((t, D_MODEL), BF16),
                   jax.ShapeDtypeStruct((t // tm // bpb, bpb, D_MODEL, tm), BF16),
                   jax.ShapeDtypeStruct((t, 2 * D_MODEL), BF16),
                   jax.ShapeDtypeStruct((t, D_MODEL), BF16)],
        compiler_params=_params(1),
        name="na_proj",
    )(x, mod, nw, w_in, q_gain, k_gain_cols)


def _na_latent_attn_kernel(q_ref, k_ref, v_ref, kc_ref, vc_ref, bias_ref, g_ref, x_ref, mod_ref,
                           wo_ref, y_ref, og_ref):
    i = pl.program_id(0)
    blk0 = i // 2
    win0 = pl.multiple_of(blk0 * NA_TQ, NA_TQ)
    first = _pair_masks()
    for p in range(NA_HEADS // 2):
        sl = slice(p * NA_PAIR, (p + 1) * NA_PAIR)
        sl2 = slice(2 * p * NA_PAIR, 2 * (p + 1) * NA_PAIR)
        kw_t = jnp.concatenate([k_ref[0, blk0 + c, sl, :] for c in range(NA_WIN // NA_TQ)], axis=1)
        vw = v_ref[0, pl.ds(win0, NA_WIN), sl2]
        kc_t, vc_t = kc_ref[0, sl, :], vc_ref[0, sl2, :]
        outs = []
        for hh, qh in enumerate(_split_pair(q_ref[:, sl], first)):
            s_loc = _dot(qh, kw_t) + bias_ref[2 * p + hh, 0].astype(F32)
            outs.append(_softmax_pv([s_loc, _dot(qh, kc_t)], [(vw, False), (vc_t, True)]))
        o2 = jnp.where(first, outs[0], outs[1])
        og_ref[:, sl] = (o2 * g_ref[:, sl].astype(F32)).astype(BF16)
    y_ref[...] = x_ref[...] + mod_ref[0][2:3] * _dot(og_ref[...], wo_ref[...])


def _na_latent_attn(q, k, v, kc, vc, bias, g, x, mod, w_out, j):
    nb, nq = k.shape[0], k.shape[1]
    n = nq * NA_TQ
    tok = pl.BlockSpec((NA_TQ, D_MODEL), lambda i, b: (b * nq + i, 0))
    full = pl.BlockSpec((1, nq, D_MODEL, NA_TQ), lambda i, b: (b, 0, 0, 0))
    full2 = pl.BlockSpec((1, n, 2 * D_MODEL), lambda i, b: (b, 0, 0))
    ctx = pl.BlockSpec((1, D_MODEL, kc.shape[2]), lambda i, b: (b, 0, 0))
    ctx2 = pl.BlockSpec((1, 2 * D_MODEL, kc.shape[2]), lambda i, b: (b, 0, 0))
    return pl.pallas_call(
        _na_latent_attn_kernel,
        grid=(nq, nb),
        in_specs=[tok, full, full2, ctx, ctx2,
                  pl.BlockSpec((NA_HEADS, 1, NA_TQ, NA_WIN), lambda i, b: (0, i, 0, 0)),
                  tok, tok,
                  pl.BlockSpec((1, 3, D_MODEL), lambda i, b: (b, 0, 0)),
                  _layer_spec((D_MODEL, D_MODEL), j)],
        out_specs=tok,
        out_shape=jax.ShapeDtypeStruct((nb * n, D_MODEL), F32),
        scratch_shapes=[pltpu.VMEM((NA_TQ, D_MODEL), BF16)],
        compiler_params=_params(2),
        name="na_latent_attn",
    )(q, k, v, kc, vc, bias, g, x, mod, w_out)


def _na_window_plan(rows):
    n_blk = rows // NA_Q_ROWS
    kh = min(NA_KH, rows)
    n_chunk = NA_WIN_ROWS // 2
    a0 = np.zeros((n_blk, NA_Q_ROWS, n_chunk), np.int64)
    row_mask = np.zeros((n_blk, NA_TQ, NA_WIN), np.float32)
    for i in range(n_blk):
        win_row0 = (i // 2) * NA_Q_ROWS
        for ql in range(NA_Q_ROWS):
            qr = i * NA_Q_ROWS + ql
            rstart = min(max(qr - kh // 2, 0), rows - kh)
            for kl in range(NA_WIN_ROWS):
                kr = win_row0 + kl
                if not rstart <= kr < rstart + kh:
                    row_mask[i, ql * GRID_W:(ql + 1) * GRID_W, kl * GRID_W:(kl + 1) * GRID_W] = NEG_INF
            for m in range(n_chunk):
                a0[i, ql, m] = win_row0 + 2 * m - qr + (NA_KH - 1)
    return a0, row_mask


def _na_bias_kernel(te_ref, to_ref, rm_ref, o_ref, *, idx):
    n_blk, n_ql, n_chunk = idx.shape
    for i in range(n_blk):
        for ql in range(n_ql):
            rs = slice(ql * GRID_W, (ql + 1) * GRID_W)
            for m in range(n_chunk):
                cs = slice(m * 128, (m + 1) * 128)
                e = int(idx[i, ql, m])
                src = te_ref if e % 2 == 0 else to_ref
                lane0 = (e - e % 2) * GRID_W
                o_ref[0, i, rs, cs] = src[0, :, lane0:lane0 + 128] + rm_ref[i, rs, cs]


def _na_bias_tables(rpb, rows):
    a0, row_mask = _na_window_plan(rows)
    n_blk = a0.shape[0]
    cq = np.arange(GRID_W)
    cstart = np.clip(cq - NA_KW // 2, 0, GRID_W - NA_KW)
    col_ok = (cq[None, :] >= cstart[:, None]) & (cq[None, :] < cstart[:, None] + NA_KW)
    coff = np.clip(cq[None, :] - cq[:, None], -(NA_KW - 1), NA_KW - 1) + (NA_KW - 1)
    onehot = (coff[None] == np.arange(2 * NA_KW - 1)[:, None, None]) & col_ok[None]
    t = jnp.einsum("hab,bqk->hqak", rpb, jnp.asarray(onehot, F32), precision=lax.Precision.HIGHEST)
    t = jnp.where(col_ok[None, :, None, :], t * LOG2E, NEG_INF)
    lo, hi = int(a0.min()), int(a0.max()) + 1
    assert lo <= 0 and (hi - lo) % 2 == 0
    n_off = 2 * NA_KH - 1
    tm = jnp.pad(t, ((0, 0), (0, 0), (-lo, max(hi - n_off + 1, 0)), (0, 0)), constant_values=NEG_INF)
    tm = tm.reshape(NA_HEADS, GRID_W, (hi - lo + 1) * GRID_W).astype(BF16)
    width = (hi - lo) * GRID_W
    tab = pl.BlockSpec((1, GRID_W, width), lambda h: (h, 0, 0))
    return pl.pallas_call(
        functools.partial(_na_bias_kernel, idx=a0 - lo),
        grid=(NA_HEADS,),
        in_specs=[tab, tab, _const_spec((n_blk, NA_TQ, NA_WIN))],
        out_specs=pl.BlockSpec((1, n_blk, NA_TQ, NA_WIN), lambda h: (h, 0, 0, 0)),
        out_shape=jax.ShapeDtypeStruct((NA_HEADS, n_blk, NA_TQ, NA_WIN), BF16),
        compiler_params=_params(1, 32 * 1024 * 1024),
        name="na_bias",
    )(tm[:, :, :width], tm[:, :, GRID_W:], jnp.asarray(row_mask, BF16))


def _ret_qkvg(x, mod, nw, w_ref, rope_refs):
    h = _norm_mod(x, nw, mod).astype(BF16)
    qw, vw = RET_QK_WIDTH, RET_WIDTH

    def rot(t):
        if rope_refs is None:
            return t
        cos, sin = rope_refs[0][...], rope_refs[1][...]
        parts = []
        for c in range(qw // 128):
            sl = slice(c * 128, (c + 1) * 128)
            tc = t[:, sl]
            parts.append(tc * cos[:, sl] + pltpu.roll(tc, 64, axis=1) * sin[:, sl])
        return jnp.concatenate(parts, axis=1)

    q = rot(_dot(h, w_ref[:, 0:qw])).astype(BF16)
    k = (rot(_dot(h, w_ref[:, qw:2 * qw])) * (RET_QK_DIM ** -0.5)).astype(BF16)
    v = _dot(h, w_ref[:, 2 * qw:2 * qw + vw]).astype(BF16)
    g = _silu(_dot(h, w_ref[:, 2 * qw + vw:2 * qw + 2 * vw]))
    return q, k, v, g


def _ret_decay_init(lg_ref, dec_ref, row0):
    _, tq, n = dec_ref.shape
    qi = (row0 + lax.broadcasted_iota(jnp.int32, (tq, n), 0)).astype(F32)
    diff = qi - lax.broadcasted_iota(jnp.int32, (tq, n), 1).astype(F32)
    for h in range(RET_HEADS):
        lgf, lgb = lg_ref[h], lg_ref[RET_HEADS + h]
        dec = jnp.exp(jnp.where(diff >= 0.0, lgf * diff, -lgb * diff))
        dec_ref[h] = jnp.where(diff == 0.0, 2.0, dec)


def _ret_mix(lg_ref, dec_ref, q, k, v, g, gn_ref, yg_ref, row0, s0_ref, st_ref):
    _, tq, n = dec_ref.shape
    for h in range(RET_HEADS):
        lgf, lgb = lg_ref[h], lg_ref[RET_HEADS + h]
        qh = q[:, h * RET_QK_DIM:(h + 1) * RET_QK_DIM]
        kh = k[:, h * RET_QK_DIM:(h + 1) * RET_QK_DIM]
        vh = v[:, h * RET_V_DIM:(h + 1) * RET_V_DIM]
        inner = (_dot_nt(qh, kh) * dec_ref[h]).astype(BF16)
        o = _dot(inner, vh)
        if s0_ref is not None:
            pos = (row0 + lax.broadcasted_iota(jnp.int32, (tq, 1), 0)).astype(F32)
            o = o + _dot(qh, s0_ref[0, 0, 0, h].astype(BF16)) * jnp.exp(lgf * (pos + 1.0))
            o = o + _dot(qh, s0_ref[0, 0, 1, h].astype(BF16)) * jnp.exp(lgb * (float(n) - pos))
        if st_ref is not None:
            kpos = lax.broadcasted_iota(jnp.int32, (n, RET_QK_DIM), 0).astype(F32)
            kf = kh.astype(F32)
            kdf = (kf * jnp.exp(lgf * (float(n - 1) - kpos))).T.astype(BF16)
            kdb = (kf * jnp.exp(lgb * kpos)).T.astype(BF16)
            st_ref[0, 0, h] = _dot(kdf, vh)
            st_ref[0, 1, h] = _dot(kdb, vh)
        mu = jnp.mean(o, axis=-1, keepdims=True)
        oc = o - mu
        var = jnp.mean(oc * oc, axis=-1, keepdims=True)
        vs = slice(h * RET_V_DIM, (h + 1) * RET_V_DIM)
        yn = oc * lax.rsqrt(var + EPS) * gn_ref[:, vs]
        yg_ref[:, vs] = (yn * g[:, vs].astype(F32)).astype(BF16)


def _ret_prompt_kernel(lg_ref, x_ref, mod_ref, nw_ref, w_ref, gn_ref, wo_ref, y_ref, st_ref, dec_ref, yg_ref):
    @pl.when(pl.program_id(0) == 0)
    def _():
        _ret_decay_init(lg_ref, dec_ref, 0)

    x = x_ref[...]
    mod = mod_ref[0]
    q, k, v, g = _ret_qkvg(x, mod, nw_ref[...], w_ref, None)
    _ret_mix(lg_ref, dec_ref, q, k, v, g, gn_ref, yg_ref, 0, None, st_ref)
    y_ref[...] = x + mod[2:3] * _dot(yg_ref[...], wo_ref[...])


def _ret_prompt_layer(lg, x, mod, nw, w_in, w_out, j, gn_w):
    t = x.shape[0]
    n = TOKEN_BLOCK
    tok = pl.BlockSpec((n, D_MODEL), lambda b: (b, 0))
    st_block = (1, 2, RET_HEADS, RET_QK_DIM, RET_V_DIM)
    return pl.pallas_call(
        _ret_prompt_kernel,
        grid=(t // n,),
        in_specs=[pl.BlockSpec(memory_space=pltpu.SMEM),
                  tok,
                  _const_spec((1, 3, D_MODEL)),
                  _const_spec((1, D_MODEL)),
                  _layer_spec((D_MODEL, 2 * RET_QK_WIDTH + 2 * RET_WIDTH), j),
                  _const_spec((1, RET_WIDTH)),
                  _layer_spec((RET_WIDTH, D_MODEL), j)],
        out_specs=[tok, pl.BlockSpec(st_block, lambda b: (b, 0, 0, 0, 0))],
        out_shape=[jax.ShapeDtypeStruct((t, D_MODEL), F32),
                   jax.ShapeDtypeStruct((t // n,) + st_block[1:], F32)],
        scratch_shapes=[pltpu.VMEM((RET_HEADS, n, n), F32), pltpu.VMEM((n, RET_WIDTH), BF16)],
        compiler_params=_params(1),
        name="ret_prompt",
    )(lg, x, mod, nw, w_in, gn_w, w_out)


def _ret_proj_kernel(x_ref, mod_ref, nw_ref, w_ref, cos_ref, sin_ref, q_out, k_out, v_out, g_out):
    q, k, v, g = _ret_qkvg(x_ref[...], mod_ref[0], nw_ref[...], w_ref, (cos_ref, sin_ref))
    q_out[...] = q
    k_out[...] = k
    v_out[...] = v
    g_out[...] = g.astype(BF16)


def _ret_proj(x, mod, nw, w_in, j, blocks_per_batch, rope_tables):
    t = x.shape[0]
    tm = TOKEN_BLOCK
    tok = pl.BlockSpec((tm, D_MODEL), lambda i: (i, 0))
    tok2 = pl.BlockSpec((tm, RET_WIDTH), lambda i: (i, 0))
    pos = pl.BlockSpec((tm, RET_QK_WIDTH), lambda i: (i % blocks_per_batch, 0))
    return pl.pallas_call(
        _ret_proj_kernel,
        grid=(t // tm,),
        in_specs=[tok,
                  pl.BlockSpec((1, 3, D_MODEL), lambda i: (i // blocks_per_batch, 0, 0)),
                  _const_spec((1, D_MODEL)),
                  _layer_spec((D_MODEL, 2 * RET_QK_WIDTH + 2 * RET_WIDTH), j),
                  pos, pos],
        out_specs=[tok, tok, tok2, tok2],
        out_shape=[jax.ShapeDtypeStruct((t, RET_QK_WIDTH), BF16)] * 2
        + [jax.ShapeDtypeStruct((t, RET_WIDTH), BF16)] * 2,
        compiler_params=_params(1),
        name="ret_proj",
    )(x, mod, nw, w_in, *rope_tables)


def _rope_tables(n):
    half = RET_QK_DIM // 2
    t = np.arange(n)
    inv = ROPE_BASE ** (-jnp.arange(0, half, 2, dtype=F32) / half)
    tabs = []
    for pos in ((t // GRID_W).astype(np.float32), (t % GRID_W).astype(np.float32)):
        ang = jnp.asarray(pos)[:, None] * inv[None, :]
        tabs.append((jnp.cos(ang), jnp.sin(ang)))
    cos = jnp.concatenate([tabs[0][0], tabs[0][0], tabs[1][0], tabs[1][0]], axis=1)
    sin = jnp.concatenate([-tabs[0][1], tabs[0][1], -tabs[1][1], tabs[1][1]], axis=1)
    return jnp.tile(cos, (1, RET_HEADS)), jnp.tile(sin, (1, RET_HEADS))


def _ret_core_kernel(lg_ref, q_ref, k_ref, v_ref, g_ref, x_ref, mod_ref, gn_ref, wo_ref, s0_ref,
                     y_ref, dec_ref, yg_ref):
    row0 = pl.program_id(0) * q_ref.shape[0]

    @pl.when(pl.program_id(1) == 0)
    def _():
        _ret_decay_init(lg_ref, dec_ref, row0)

    _ret_mix(lg_ref, dec_ref, q_ref, k_ref.at[0], v_ref.at[0], g_ref, gn_ref, yg_ref, row0, s0_ref, None)
    y_ref[...] = x_ref[...] + mod_ref[0][2:3] * _dot(yg_ref[...], wo_ref[...])


def _ret_core(lg, q, k, v, g, x, mod, gn_w, w_out, j, n, s0):
    t = x.shape[0]
    nb = t // n
    tq = TOKEN_BLOCK
    nq = n // tq
    tok = pl.BlockSpec((tq, D_MODEL), lambda i, b: (b * nq + i, 0))
    tok2 = pl.BlockSpec((tq, RET_WIDTH), lambda i, b: (b * nq + i, 0))
    st_block = (1, 1, 2, RET_HEADS, RET_QK_DIM, RET_V_DIM)
    return pl.pallas_call(
        _ret_core_kernel,
        grid=(nq, nb),
        in_specs=[pl.BlockSpec(memory_space=pltpu.SMEM),
                  tok,
                  pl.BlockSpec((1, n, RET_QK_WIDTH), lambda i, b: (b, 0, 0)),
                  pl.BlockSpec((1, n, RET_WIDTH), lambda i, b: (b, 0, 0)),
                  tok2, tok,
                  pl.BlockSpec((1, 3, D_MODEL), lambda i, b: (b, 0, 0)),
                  _const_spec((1, RET_WIDTH)),
                  _layer_spec((RET_WIDTH, D_MODEL), j),
                  pl.BlockSpec(st_block, lambda i, b: (b, j, 0, 0, 0, 0))],
        out_specs=tok,
        out_shape=jax.ShapeDtypeStruct((t, D_MODEL), F32),
        scratch_shapes=[pltpu.VMEM((RET_HEADS, tq, n), F32), pltpu.VMEM((tq, RET_WIDTH), BF16)],
        compiler_params=_params(2),
        name="ret_core",
    )(lg, q, k.reshape(nb, n, RET_QK_WIDTH), v.reshape(nb, n, RET_WIDTH), g, x, mod, gn_w, w_out, s0)


def _mlp_kernel(x_ref, mod_ref, nw_ref, wi_ref, lnw_ref, lnb_ref, ws_ref, bs_ref, wo_ref, y_ref, o_ref):
    x = x_ref[...]
    mod = mod_ref[0]
    h = _norm_mod(x, nw_ref[...], mod).astype(BF16)
    w = MLP_WIDTH
    v = _gelu_tanh(_dot(h, wi_ref[:, w:2 * w]))
    mu = jnp.mean(v, axis=-1, keepdims=True)
    vc = v - mu
    var = jnp.mean(vc * vc, axis=-1, keepdims=True)
    vn = (vc * lax.rsqrt(var + EPS) * lnw_ref[...] + lnb_ref[...]).astype(BF16)
    u = _gelu_tanh(_dot(h, wi_ref[:, 0:w]))
    ug = u * _silu(_dot(h, wi_ref[:, 2 * w:3 * w]))
    tm = x.shape[0]
    for c in range(tm // MLP_CHUNK):
        rs = slice(c * MLP_CHUNK, (c + 1) * MLP_CHUNK)
        for g in range(MLP_GROUPS):
            cs = slice(g * MLP_GROUP_DIM, (g + 1) * MLP_GROUP_DIM)
            sv = _dot(ws_ref[g], vn[rs, cs]) + bs_ref[:, cs]
            o_ref[rs, cs] = (ug[rs, cs] * sv).astype(BF16)
    y_ref[...] = x + mod[2:3] * _dot(o_ref[...], wo_ref[...])


def _mlp_layer(x, mod, nw, w_in, ln_w, ln_b, w_s, b_s_cols, w_out, j, blocks_per_batch):
    t = x.shape[0]
    tm = TOKEN_BLOCK
    tok = pl.BlockSpec((tm, D_MODEL), lambda i: (i, 0))
    return pl.pallas_call(
        _mlp_kernel,
        grid=(t // tm,),
        in_specs=[tok,
                  pl.BlockSpec((1, 3, D_MODEL), lambda i: (i // blocks_per_batch, 0, 0)),
                  _const_spec((1, D_MODEL)),
                  _layer_spec((D_MODEL, 3 * MLP_WIDTH), j),
                  _const_spec((1, MLP_WIDTH)),
                  _const_spec((1, MLP_WIDTH)),
                  _layer_spec((MLP_GROUPS, MLP_CHUNK, MLP_CHUNK), j),
                  _const_spec((MLP_CHUNK, MLP_WIDTH)),
                  _layer_spec((MLP_WIDTH, D_MODEL), j)],
        out_specs=tok,
        out_shape=jax.ShapeDtypeStruct((t, D_MODEL), F32),
        scratch_shapes=[pltpu.VMEM((tm, MLP_WIDTH), BF16)],
        compiler_params=_params(1),
        name="mlp_layer",
    )(x, mod, nw, w_in, ln_w, ln_b, w_s, b_s_cols, w_out)


def kernel(x_prompt, x_sample, cache_na_k, cache_na_v, state_ret, c, c_ctx, norm_w, w_ada, b_ada,
           na_w_in, na_w_out, na_q_gain, na_k_gain, na_rpb,
           ret_w_in, ret_w_out, ret_decay_logit, ret_gn_w,
           mlp_w_in, mlp_ln_w, mlp_ln_b, mlp_w_s, mlp_b_s, mlp_w_out):
    batch, seq, _ = x_prompt.shape
    dec_batch, dec_seq, _ = x_sample.shape
    past = cache_na_k.shape[2]
    assert seq == TOKEN_BLOCK and dec_seq == 16 * GRID_W and dec_batch <= 7

    c_rows = jnp.zeros((8, D_MODEL), F32).at[:dec_batch].set(c).at[dec_batch].set(c_ctx)
    mods = _ada_all(c_rows, w_ada, b_ada).reshape(DEPTH, 8, 3, D_MODEL)

    rope_tabs = _rope_tables(dec_seq)
    na_w_in, na_w_out = na_w_in.astype(BF16), na_w_out.astype(BF16)
    ret_w_in, ret_w_out = ret_w_in.astype(BF16), ret_w_out.astype(BF16)
    mlp_w_in, mlp_w_out, mlp_w_s = mlp_w_in.astype(BF16), mlp_w_out.astype(BF16), mlp_w_s.astype(BF16)

    yp = x_prompt.reshape(batch * seq, D_MODEL)
    ys = x_sample.reshape(dec_batch * dec_seq, D_MODEL)
    bpb_s = dec_seq // TOKEN_BLOCK
    n_na = (DEPTH + 2) // N_MIXERS
    new_kv, new_s = None, []
    for i in range(DEPTH):
        kind, j = i % N_MIXERS, i // N_MIXERS
        mod_p = mods[i, dec_batch:dec_batch + 1]
        mod_s = mods[i, :dec_batch]
        nw = norm_w[i].reshape(1, D_MODEL)
        if kind == 0:
            qg = jnp.tile(na_q_gain[j], NA_HEADS).reshape(1, D_MODEL)
            kg = jnp.broadcast_to(jnp.tile(na_k_gain[j], NA_HEADS)[:, None], (D_MODEL, 128))
            yp, *new_kv = _na_prompt_layer(yp, mod_p, nw, na_w_in, na_w_out, j, qg, kg, n_na, new_kv)
            q, k, v, g = _na_proj(ys, mod_s, nw, na_w_in, j, qg, kg, bpb_s)
            kc = cache_na_k[:, j].transpose(0, 2, 3, 1).reshape(dec_batch, D_MODEL, past).astype(BF16)
            vc = cache_na_v[:, j].transpose(0, 2, 3, 1).astype(BF16)
            vc = jnp.concatenate([vc.reshape(dec_batch, NA_HEADS // 2, NA_PAIR, past),
                                  jnp.ones((dec_batch, NA_HEADS // 2, NA_PAIR, past), BF16)],
                                 axis=2).reshape(dec_batch, 2 * D_MODEL, past)
            ys = _na_latent_attn(q, k,
                                 v.reshape(dec_batch, dec_seq, 2 * D_MODEL), kc, vc,
                                 _na_bias_tables(na_rpb[j], dec_seq // GRID_W), g, ys, mod_s, na_w_out, j)
        elif kind == 1:
            lg = jax.nn.log_sigmoid(ret_decay_logit[j].astype(F32)).reshape(2 * RET_HEADS)
            gn = ret_gn_w[j].reshape(1, RET_WIDTH)
            yp, st = _ret_prompt_layer(lg, yp, mod_p, nw, ret_w_in, ret_w_out, j, gn)
            new_s.append(st)
            q, k, v, g = _ret_proj(ys, mod_s, nw, ret_w_in, j, bpb_s, rope_tabs)
            ys = _ret_core(lg, q, k, v, g, ys, mod_s, gn, ret_w_out, j, dec_seq, state_ret)
        else:
            lnw = mlp_ln_w[j].reshape(1, MLP_WIDTH)
            lnb = mlp_ln_b[j].reshape(1, MLP_WIDTH)
            b_cols = jnp.repeat(mlp_b_s[j].T, MLP_GROUP_DIM, axis=1)
            yp = _mlp_layer(yp, mod_p, nw, mlp_w_in, lnw, lnb, mlp_w_s, b_cols, mlp_w_out, j, batch)
            ys = _mlp_layer(ys, mod_s, nw, mlp_w_in, lnw, lnb, mlp_w_s, b_cols, mlp_w_out, j, bpb_s)
    return (yp.reshape(batch, seq, D_MODEL),
            ys.reshape(dec_batch, dec_seq, D_MODEL),
            *(t.reshape(batch, n_na, NA_HEADS, NA_HEAD_DIM, seq).transpose(0, 1, 4, 2, 3) for t in new_kv),
            jnp.stack(new_s, axis=1))
```

```python
import functools
import math

import numpy as np
import jax
import jax.numpy as jnp
from jax import lax
from jax.experimental import pallas as pl
from jax.experimental.pallas import tpu as pltpu

F32 = jnp.float32
BF16 = jnp.bfloat16

D_MODEL = 1024
DEPTH = 4
N_MIXERS = 3
GRID_W = 64
EPS = 1e-6
NEG_INF = -1e30
LOG2E = 1.4426950408889634

NA_HEADS = 16
NA_HEAD_DIM = 64
NA_PAIR = 2 * NA_HEAD_DIM
NA_KH = 8
NA_KW = 16
NA_Q_ROWS = 4
NA_WIN_ROWS = 12
NA_TQ = NA_Q_ROWS * GRID_W
NA_WIN = NA_WIN_ROWS * GRID_W

RET_HEADS = 4
RET_QK_DIM = 256
RET_V_DIM = 512
RET_QK_WIDTH = RET_HEADS * RET_QK_DIM
RET_WIDTH = RET_HEADS * RET_V_DIM
ROPE_BASE = 10000.0

MLP_WIDTH = 2048
MLP_GROUPS = 8
MLP_GROUP_DIM = 256
MLP_CHUNK = 128

TOKEN_BLOCK = 256
MLP_TOKEN_BLOCK = 512
V7X_VMEM_LIMIT = 56 * 1024 * 1024

_NT = (((1,), (1,)), ((), ()))


def _dot(a, b):
    return jnp.dot(a, b, preferred_element_type=F32)


def _dot_nt(a, b):
    return lax.dot_general(a, b, _NT, preferred_element_type=F32)


def _silu(x):
    return x * jax.nn.sigmoid(x)


def _gelu_tanh(x):
    c = math.sqrt(2.0 / math.pi)
    return x * (0.5 * (1.0 + jnp.tanh(c * (x + 0.044715 * (x * x * x)))))


def _norm_mod(x, nw, mod):
    ms = jnp.mean(x * x, axis=-1, keepdims=True)
    y = x * lax.rsqrt(ms + EPS) * nw
    return y * (1.0 + mod[1:2]) + mod[0:1]


def _const_spec(shape):
    nd = len(shape)
    return pl.BlockSpec(shape, lambda *_: (0,) * nd, pipeline_mode=pl.Buffered(1))


def _layer_spec(shape, j):
    nd = len(shape)
    return pl.BlockSpec((None,) + tuple(shape), lambda *_: (j,) + (0,) * nd, pipeline_mode=pl.Buffered(1))


def _params(n_axes, vmem=V7X_VMEM_LIMIT):
    return pltpu.CompilerParams(dimension_semantics=("arbitrary",) * n_axes,
                                vmem_limit_bytes=vmem)


def _ada_kernel(c_ref, w_ref, b_ref, o_ref):
    a = _silu(c_ref[...]).astype(BF16)
    o_ref[0] = _dot(a, w_ref[0].astype(BF16)) + b_ref[0]


def _ada_all(c_rows, w_ada, b_ada):
    n_col = 3
    return pl.pallas_call(
        _ada_kernel,
        grid=(DEPTH, n_col),
        in_specs=[
            pl.BlockSpec((8, D_MODEL), lambda i, n: (0, 0)),
            pl.BlockSpec((1, D_MODEL, D_MODEL), lambda i, n: (i, 0, n)),
            pl.BlockSpec((1, 1, D_MODEL), lambda i, n: (i, 0, n)),
        ],
        out_specs=pl.BlockSpec((1, 8, D_MODEL), lambda i, n: (i, 0, n)),
        out_shape=jax.ShapeDtypeStruct((DEPTH, 8, 3 * D_MODEL), F32),
        compiler_params=_params(2, 32 * 1024 * 1024),
        name="adaln",
    )(c_rows, w_ada, b_ada.reshape(DEPTH, 1, 3 * D_MODEL))


def _head_rms_t(t_t):
    slabs = []
    for h in range(NA_HEADS):
        blk = t_t[h * NA_HEAD_DIM:(h + 1) * NA_HEAD_DIM, :]
        ms = jnp.mean(blk * blk, axis=0, keepdims=True)
        slabs.append(blk * lax.rsqrt(ms + EPS))
    return jnp.concatenate(slabs, axis=0)


def _na_qkvg(x, mod, nw, w_ref, qg, kg_cols):
    h = _norm_mod(x, nw, mod).astype(BF16)
    w = D_MODEL
    qn = _head_rms_t(_dot(h, w_ref[:, 0:w]).T).T * qg
    q = (qn * (LOG2E * NA_HEAD_DIM ** -0.5)).astype(BF16)
    kn_t = _head_rms_t(_dot(h, w_ref[:, w:2 * w]).T)
    kn_t = kn_t * jnp.concatenate([kg_cols] * (kn_t.shape[1] // kg_cols.shape[1]), axis=1)
    v = _dot(h, w_ref[:, 2 * w:3 * w])
    g = _silu(_dot(h, w_ref[:, 3 * w:4 * w]))
    return q, kn_t, v, g


def _pair_masks():
    lane = lax.broadcasted_iota(jnp.int32, (1, NA_PAIR), 1)
    return lane < NA_HEAD_DIM


def _split_pair(q2, first):
    zero = jnp.zeros_like(q2)
    return jnp.where(first, q2, zero), jnp.where(first, zero, q2)


def _softmax_pv(s_parts, v_parts):
    m = functools.reduce(jnp.maximum, [jnp.max(s, axis=-1, keepdims=True) for s in s_parts])
    acc = None
    for s, (v, v_is_t) in zip(s_parts, v_parts):
        e = jnp.exp2(s - m).astype(BF16)
        o = _dot_nt(e, v) if v_is_t else _dot(e, v)
        acc = o if acc is None else acc + o
    return acc[:, :NA_PAIR] / acc[:, NA_PAIR:]


def _na_prompt_kernel(x_ref, mod_ref, nw_ref, w_ref, qg_ref, kg_ref, wo_ref, *rest, n_alias):
    y_ref, kt_ref, vt_ref, og_ref = rest[n_alias:]
    x = x_ref[...]
    mod = mod_ref[0]
    q, kn_t, v, g = _na_qkvg(x, mod, nw_ref[...], w_ref, qg_ref[...], kg_ref[...])
    kt_ref[0, 0] = kn_t
    vt_ref[0, 0] = v.T
    kb_t, vb = kn_t.astype(BF16), v.astype(BF16)
    ones = jnp.ones((x.shape[0], NA_PAIR), BF16)
    first = _pair_masks()
    for p in range(NA_HEADS // 2):
        sl = slice(p * NA_PAIR, (p + 1) * NA_PAIR)
        k2_t = kb_t[sl, :]
        v2 = jnp.concatenate([vb[:, sl], ones], axis=1)
        outs = [_softmax_pv([_dot(qh, k2_t)], [(v2, False)]) for qh in _split_pair(q[:, sl], first)]
        og_ref[:, sl] = (jnp.where(first, outs[0], outs[1]) * g[:, sl]).astype(BF16)
    y_ref[...] = x + mod[2:3] * _dot(og_ref[...], wo_ref[...])


def _na_prompt_layer(x, mod, nw, w_in, w_out, j, q_gain, k_gain_cols, n_layers, cache_prev):
    t = x.shape[0]
    tm = TOKEN_BLOCK
    tok = pl.BlockSpec((tm, D_MODEL), lambda i: (i, 0))
    in_specs = [tok,
                _const_spec((1, 3, D_MODEL)),
                _const_spec((1, D_MODEL)),
                _layer_spec((D_MODEL, 4 * D_MODEL), j),
                _const_spec((1, D_MODEL)),
                _const_spec((D_MODEL, 128)),
                _layer_spec((D_MODEL, D_MODEL), j)]
    args = [x, mod, nw, w_in, q_gain, k_gain_cols, w_out]
    cache = pl.BlockSpec((1, 1, D_MODEL, tm), lambda i: (i, j, 0, 0))
    cache_shape = jax.ShapeDtypeStruct((t // tm, n_layers, D_MODEL, tm), F32)
    aliases = {}
    if cache_prev is not None:
        aliases = {len(args): 1, len(args) + 1: 2}
        in_specs += [pl.BlockSpec(memory_space=pl.ANY)] * 2
        args += list(cache_prev)
    return pl.pallas_call(
        functools.partial(_na_prompt_kernel, n_alias=len(aliases)),
        grid=(t // tm,),
        in_specs=in_specs,
        out_specs=[tok, cache, cache],
        out_shape=[jax.ShapeDtypeStruct((t, D_MODEL), F32), cache_shape, cache_shape],
        input_output_aliases=aliases,
        scratch_shapes=[pltpu.VMEM((tm, D_MODEL), BF16)],
        compiler_params=_params(1),
        name="na_prompt",
    )(*args)


def _na_proj_kernel(x_ref, mod_ref, nw_ref, w_ref, qg_ref, kg_ref, q_out, kt_out, v_out, g_out):
    q, kn_t, v, g = _na_qkvg(x_ref[...], mod_ref[0], nw_ref[...], w_ref, qg_ref[...], kg_ref[...])
    q_out[...] = q
    kt_out[0, 0] = kn_t.astype(BF16)
    g_out[...] = g.astype(BF16)
    vb = v.astype(BF16)
    ones = jnp.ones((vb.shape[0], NA_PAIR), BF16)
    for p in range(NA_HEADS // 2):
        v_out[:, 2 * p * NA_PAIR:(2 * p + 1) * NA_PAIR] = vb[:, p * NA_PAIR:(p + 1) * NA_PAIR]
        v_out[:, (2 * p + 1) * NA_PAIR:(2 * p + 2) * NA_PAIR] = ones


def _na_proj(x, mod, nw, w_in, j, q_gain, k_gain_cols, blocks_per_batch):
    t = x.shape[0]
    tm = TOKEN_BLOCK
    bpb = blocks_per_batch
    tok = pl.BlockSpec((tm, D_MODEL), lambda i: (i, 0))
    tok_v = pl.BlockSpec((tm, 2 * D_MODEL), lambda i: (i, 0))
    return pl.pallas_call(
        _na_proj_kernel,
        grid=(t // tm,),
        in_specs=[tok,
                  pl.BlockSpec((1, 3, D_MODEL), lambda i: (i // bpb, 0, 0)),
                  _const_spec((1, D_MODEL)),
                  _layer_spec((D_MODEL, 4 * D_MODEL), j),
                  _const_spec((1, D_MODEL)),
                  _const_spec((D_MODEL, 128))],
        out_specs=[tok, pl.BlockSpec((1, 1, D_MODEL, tm), lambda i: (i // bpb, i % bpb, 0, 0)), tok_v, tok],
        out_shape=[jax.ShapeDtypeStruct((t, D_MODEL), BF16),
                   jax.ShapeDtypeStruct((t // tm // bpb, bpb, D_MODEL, tm), BF16),
                   jax.ShapeDtypeStruct((t, 2 * D_MODEL), BF16),
                   jax.ShapeDtypeStruct((t, D_MODEL), BF16)],
        compiler_params=_params(1),
        name="na_proj",
    )(x, mod, nw, w_in, q_gain, k_gain_cols)


def _na_latent_attn_kernel(q_ref, k_ref, v_ref, kc_ref, vc_ref, bias_ref, g_ref, x_ref, mod_ref,
                           wo_ref, y_ref, og_ref):
    i = pl.program_id(0)
    blk0 = i // 2
    win0 = pl.multiple_of(blk0 * NA_TQ, NA_TQ)
    first = _pair_masks()
    for p in range(NA_HEADS // 2):
        sl = slice(p * NA_PAIR, (p + 1) * NA_PAIR)
        sl2 = slice(2 * p * NA_PAIR, 2 * (p + 1) * NA_PAIR)
        kw_t = jnp.concatenate([k_ref[0, blk0 + c, sl, :] for c in range(NA_WIN // NA_TQ)], axis=1)
        vw = v_ref[0, pl.ds(win0, NA_WIN), sl2]
        kc_t, vc_t = kc_ref[0, sl, :], vc_ref[0, sl2, :]
        outs = []
        for hh, qh in enumerate(_split_pair(q_ref[:, sl], first)):
            s_loc = _dot(qh, kw_t) + bias_ref[2 * p + hh, 0].astype(F32)
            outs.append(_softmax_pv([s_loc, _dot(qh, kc_t)], [(vw, False), (vc_t, True)]))
        o2 = jnp.where(first, outs[0], outs[1])
        og_ref[:, sl] = (o2 * g_ref[:, sl].astype(F32)).astype(BF16)
    y_ref[...] = x_ref[...] + mod_ref[0][2:3] * _dot(og_ref[...], wo_ref[...])


def _na_latent_attn(q, k, v, kc, vc, bias, g, x, mod, w_out, j):
    nb, nq = k.shape[0], k.shape[1]
    n = nq * NA_TQ
    tok = pl.BlockSpec((NA_TQ, D_MODEL), lambda i, b: (b * nq + i, 0))
    full = pl.BlockSpec((1, nq, D_MODEL, NA_TQ), lambda i, b: (b, 0, 0, 0))
    full2 = pl.BlockSpec((1, n, 2 * D_MODEL), lambda i, b: (b, 0, 0))
    ctx = pl.BlockSpec((1, D_MODEL, kc.shape[2]), lambda i, b: (b, 0, 0))
    ctx2 = pl.BlockSpec((1, 2 * D_MODEL, kc.shape[2]), lambda i, b: (b, 0, 0))
    return pl.pallas_call(
        _na_latent_attn_kernel,
        grid=(nq, nb),
        in_specs=[tok, full, full2, ctx, ctx2,
                  pl.BlockSpec((NA_HEADS, 1, NA_TQ, NA_WIN), lambda i, b: (0, i, 0, 0)),
                  tok, tok,
                  pl.BlockSpec((1, 3, D_MODEL), lambda i, b: (b, 0, 0)),
                  _layer_spec((D_MODEL, D_MODEL), j)],
        out_specs=tok,
        out_shape=jax.ShapeDtypeStruct((nb * n, D_MODEL), F32),
        scratch_shapes=[pltpu.VMEM((NA_TQ, D_MODEL), BF16)],
        compiler_params=_params(2),
        name="na_latent_attn",
    )(q, k, v, kc, vc, bias, g, x, mod, w_out)


def _na_window_plan(rows):
    n_blk = rows // NA_Q_ROWS
    kh = min(NA_KH, rows)
    n_chunk = NA_WIN_ROWS // 2
    a0 = np.zeros((n_blk, NA_Q_ROWS, n_chunk), np.int64)
    row_mask = np.zeros((n_blk, NA_TQ, NA_WIN), np.float32)
    for i in range(n_blk):
        win_row0 = (i // 2) * NA_Q_ROWS
        for ql in range(NA_Q_ROWS):
            qr = i * NA_Q_ROWS + ql
            rstart = min(max(qr - kh // 2, 0), rows - kh)
            for kl in range(NA_WIN_ROWS):
                kr = win_row0 + kl
                if not rstart <= kr < rstart + kh:
                    row_mask[i, ql * GRID_W:(ql + 1) * GRID_W, kl * GRID_W:(kl + 1) * GRID_W] = NEG_INF
            for m in range(n_chunk):
                a0[i, ql, m] = win_row0 + 2 * m - qr + (NA_KH - 1)
    return a0, row_mask


def _na_bias_kernel(t_ref, cm_ref, rm_ref, o_ref, *, a0):
    n_off = t_ref.shape[1]
    outside = jnp.full((GRID_W, GRID_W), NEG_INF, F32)

    def tile(a):
        return t_ref[0, a] * LOG2E if 0 <= a < n_off else outside

    pairs = {a: jnp.concatenate([tile(a), tile(a + 1)], axis=1) + cm_ref[...]
             for a in sorted(set(int(a) for a in a0.flatten()))}
    n_blk, n_ql, n_chunk = a0.shape
    for i in range(n_blk):
        for ql in range(n_ql):
            rs = slice(ql * GRID_W, (ql + 1) * GRID_W)
            for m in range(n_chunk):
                cs = slice(m * 128, (m + 1) * 128)
                o_ref[0, i, rs, cs] = (pairs[int(a0[i, ql, m])] + rm_ref[i, rs, cs].astype(F32)).astype(BF16)


def _na_bias_tables(rpb, rows):
    a0, row_mask = _na_window_plan(rows)
    n_blk = a0.shape[0]
    n_off = 2 * NA_KH - 1
    cq = np.arange(GRID_W)
    cstart = np.clip(cq - NA_KW // 2, 0, GRID_W - NA_KW)
    col_ok = (cq[None, :] >= cstart[:, None]) & (cq[None, :] < cstart[:, None] + NA_KW)
    coff = np.clip(cq[None, :] - cq[:, None], -(NA_KW - 1), NA_KW - 1) + (NA_KW - 1)
    onehot = (coff[None] == np.arange(2 * NA_KW - 1)[:, None, None]) & col_ok[None]
    t = jnp.einsum("hab,bqk->haqk", rpb, jnp.asarray(onehot, F32), precision=lax.Precision.HIGHEST)
    col_mask = np.where(np.concatenate([col_ok, col_ok], axis=1), 0.0, NEG_INF).astype(np.float32)
    return pl.pallas_call(
        functools.partial(_na_bias_kernel, a0=a0),
        grid=(NA_HEADS,),
        in_specs=[pl.BlockSpec((1, n_off, GRID_W, GRID_W), lambda h: (h, 0, 0, 0)),
                  _const_spec((GRID_W, 128)),
                  _const_spec((n_blk, NA_TQ, NA_WIN))],
        out_specs=pl.BlockSpec((1, n_blk, NA_TQ, NA_WIN), lambda h: (h, 0, 0, 0)),
        out_shape=jax.ShapeDtypeStruct((NA_HEADS, n_blk, NA_TQ, NA_WIN), BF16),
        compiler_params=_params(1, 32 * 1024 * 1024),
        name="na_bias",
    )(t, jnp.asarray(col_mask), jnp.asarray(row_mask, BF16))


def _ret_qkvg(x, mod, nw, w_ref, rope_refs):
    h = _norm_mod(x, nw, mod).astype(BF16)
    qw, vw = RET_QK_WIDTH, RET_WIDTH

    def rot(t):
        if rope_refs is None:
            return t
        cos, sin = rope_refs[0][...], rope_refs[1][...]
        parts = []
        for c in range(qw // 128):
            sl = slice(c * 128, (c + 1) * 128)
            tc = t[:, sl]
            parts.append(tc * cos[:, sl] + pltpu.roll(tc, 64, axis=1) * sin[:, sl])
        return jnp.concatenate(parts, axis=1)

    q = rot(_dot(h, w_ref[:, 0:qw])).astype(BF16)
    k = (rot(_dot(h, w_ref[:, qw:2 * qw])) * (RET_QK_DIM ** -0.5)).astype(BF16)
    v = _dot(h, w_ref[:, 2 * qw:2 * qw + vw]).astype(BF16)
    g = _silu(_dot(h, w_ref[:, 2 * qw + vw:2 * qw + 2 * vw]))
    return q, k, v, g


def _ret_decay_init(lg_ref, dec_ref, row0):
    _, tq, n = dec_ref.shape
    qi = (row0 + lax.broadcasted_iota(jnp.int32, (tq, n), 0)).astype(F32)
    diff = qi - lax.broadcasted_iota(jnp.int32, (tq, n), 1).astype(F32)
    for h in range(RET_HEADS):
        lgf, lgb = lg_ref[h], lg_ref[RET_HEADS + h]
        dec = jnp.exp(jnp.where(diff >= 0.0, lgf * diff, -lgb * diff))
        dec_ref[h] = jnp.where(diff == 0.0, 2.0, dec)


def _ret_mix(lg_ref, dec_ref, q, k, v, g, gn_ref, yg_ref, row0, s0_ref, st_ref):
    _, tq, n = dec_ref.shape
    for h in range(RET_HEADS):
        lgf, lgb = lg_ref[h], lg_ref[RET_HEADS + h]
        qh = q[:, h * RET_QK_DIM:(h + 1) * RET_QK_DIM]
        kh = k[:, h * RET_QK_DIM:(h + 1) * RET_QK_DIM]
        vh = v[:, h * RET_V_DIM:(h + 1) * RET_V_DIM]
        inner = (_dot_nt(qh, kh) * dec_ref[h]).astype(BF16)
        o = _dot(inner, vh)
        if s0_ref is not None:
            pos = (row0 + lax.broadcasted_iota(jnp.int32, (tq, 1), 0)).astype(F32)
            o = o + _dot(qh, s0_ref[0, 0, 0, h].astype(BF16)) * jnp.exp(lgf * (pos + 1.0))
            o = o + _dot(qh, s0_ref[0, 0, 1, h].astype(BF16)) * jnp.exp(lgb * (float(n) - pos))
        if st_ref is not None:
            kpos = lax.broadcasted_iota(jnp.int32, (n, RET_QK_DIM), 0).astype(F32)
            kf = kh.astype(F32)
            kdf = (kf * jnp.exp(lgf * (float(n - 1) - kpos))).T.astype(BF16)
            kdb = (kf * jnp.exp(lgb * kpos)).T.astype(BF16)
            st_ref[0, 0, h] = _dot(kdf, vh)
            st_ref[0, 1, h] = _dot(kdb, vh)
        mu = jnp.mean(o, axis=-1, keepdims=True)
        oc = o - mu
        var = jnp.mean(oc * oc, axis=-1, keepdims=True)
        vs = slice(h * RET_V_DIM, (h + 1) * RET_V_DIM)
        yn = oc * lax.rsqrt(var + EPS) * gn_ref[:, vs]
        yg_ref[:, vs] = (yn * g[:, vs].astype(F32)).astype(BF16)


def _ret_prompt_kernel(lg_ref, x_ref, mod_ref, nw_ref, w_ref, gn_ref, wo_ref, y_ref, st_ref, dec_ref, yg_ref):
    @pl.when(pl.program_id(0) == 0)
    def _():
        _ret_decay_init(lg_ref, dec_ref, 0)

    x = x_ref[...]
    mod = mod_ref[0]
    q, k, v, g = _ret_qkvg(x, mod, nw_ref[...], w_ref, None)
    _ret_mix(lg_ref, dec_ref, q, k, v, g, gn_ref, yg_ref, 0, None, st_ref)
    y_ref[...] = x + mod[2:3] * _dot(yg_ref[...], wo_ref[...])


def _ret_prompt_layer(lg, x, mod, nw, w_in, w_out, j, gn_w):
    t = x.shape[0]
    n = TOKEN_BLOCK
    tok = pl.BlockSpec((n, D_MODEL), lambda b: (b, 0))
    st_block = (1, 2, RET_HEADS, RET_QK_DIM, RET_V_DIM)
    return pl.pallas_call(
        _ret_prompt_kernel,
        grid=(t // n,),
        in_specs=[pl.BlockSpec(memory_space=pltpu.SMEM),
                  tok,
                  _const_spec((1, 3, D_MODEL)),
                  _const_spec((1, D_MODEL)),
                  _layer_spec((D_MODEL, 2 * RET_QK_WIDTH + 2 * RET_WIDTH), j),
                  _const_spec((1, RET_WIDTH)),
                  _layer_spec((RET_WIDTH, D_MODEL), j)],
        out_specs=[tok, pl.BlockSpec(st_block, lambda b: (b, 0, 0, 0, 0))],
        out_shape=[jax.ShapeDtypeStruct((t, D_MODEL), F32),
                   jax.ShapeDtypeStruct((t // n,) + st_block[1:], F32)],
        scratch_shapes=[pltpu.VMEM((RET_HEADS, n, n), F32), pltpu.VMEM((n, RET_WIDTH), BF16)],
        compiler_params=_params(1),
        name="ret_prompt",
    )(lg, x, mod, nw, w_in, gn_w, w_out)


def _ret_proj_kernel(x_ref, mod_ref, nw_ref, w_ref, cos_ref, sin_ref, q_out, k_out, v_out, g_out):
    q, k, v, g = _ret_qkvg(x_ref[...], mod_ref[0], nw_ref[...], w_ref, (cos_ref, sin_ref))
    q_out[...] = q
    k_out[...] = k
    v_out[...] = v
    g_out[...] = g.astype(BF16)


def _ret_proj(x, mod, nw, w_in, j, blocks_per_batch, rope_tables):
    t = x.shape[0]
    tm = TOKEN_BLOCK
    tok = pl.BlockSpec((tm, D_MODEL), lambda i: (i, 0))
    tok2 = pl.BlockSpec((tm, RET_WIDTH), lambda i: (i, 0))
    pos = pl.BlockSpec((tm, RET_QK_WIDTH), lambda i: (i % blocks_per_batch, 0))
    return pl.pallas_call(
        _ret_proj_kernel,
        grid=(t // tm,),
        in_specs=[tok,
                  pl.BlockSpec((1, 3, D_MODEL), lambda i: (i // blocks_per_batch, 0, 0)),
                  _const_spec((1, D_MODEL)),
                  _layer_spec((D_MODEL, 2 * RET_QK_WIDTH + 2 * RET_WIDTH), j),
                  pos, pos],
        out_specs=[tok, tok, tok2, tok2],
        out_shape=[jax.ShapeDtypeStruct((t, RET_QK_WIDTH), BF16)] * 2
        + [jax.ShapeDtypeStruct((t, RET_WIDTH), BF16)] * 2,
        compiler_params=_params(1),
        name="ret_proj",
    )(x, mod, nw, w_in, *rope_tables)


def _rope_tables(n):
    half = RET_QK_DIM // 2
    t = np.arange(n)
    inv = ROPE_BASE ** (-jnp.arange(0, half, 2, dtype=F32) / half)
    tabs = []
    for pos in ((t // GRID_W).astype(np.float32), (t % GRID_W).astype(np.float32)):
        ang = jnp.asarray(pos)[:, None] * inv[None, :]
        tabs.append((jnp.cos(ang), jnp.sin(ang)))
    cos = jnp.concatenate([tabs[0][0], tabs[0][0], tabs[1][0], tabs[1][0]], axis=1)
    sin = jnp.concatenate([-tabs[0][1], tabs[0][1], -tabs[1][1], tabs[1][1]], axis=1)
    return jnp.tile(cos, (1, RET_HEADS)), jnp.tile(sin, (1, RET_HEADS))


def _ret_core_kernel(lg_ref, q_ref, k_ref, v_ref, g_ref, x_ref, mod_ref, gn_ref, wo_ref, s0_ref,
                     y_ref, dec_ref, yg_ref):
    row0 = pl.program_id(0) * q_ref.shape[0]

    @pl.when(pl.program_id(1) == 0)
    def _():
        _ret_decay_init(lg_ref, dec_ref, row0)

    _ret_mix(lg_ref, dec_ref, q_ref, k_ref.at[0], v_ref.at[0], g_ref, gn_ref, yg_ref, row0, s0_ref, None)
    y_ref[...] = x_ref[...] + mod_ref[0][2:3] * _dot(yg_ref[...], wo_ref[...])


def _ret_core(lg, q, k, v, g, x, mod, gn_w, w_out, j, n, s0):
    t = x.shape[0]
    nb = t // n
    tq = TOKEN_BLOCK
    nq = n // tq
    tok = pl.BlockSpec((tq, D_MODEL), lambda i, b: (b * nq + i, 0))
    tok2 = pl.BlockSpec((tq, RET_WIDTH), lambda i, b: (b * nq + i, 0))
    st_block = (1, 1, 2, RET_HEADS, RET_QK_DIM, RET_V_DIM)
    return pl.pallas_call(
        _ret_core_kernel,
        grid=(nq, nb),
        in_specs=[pl.BlockSpec(memory_space=pltpu.SMEM),
                  tok,
                  pl.BlockSpec((1, n, RET_QK_WIDTH), lambda i, b: (b, 0, 0)),
                  pl.BlockSpec((1, n, RET_WIDTH), lambda i, b: (b, 0, 0)),
                  tok2, tok,
                  pl.BlockSpec((1, 3, D_MODEL), lambda i, b: (b, 0, 0)),
                  _const_spec((1, RET_WIDTH)),
                  _layer_spec((RET_WIDTH, D_MODEL), j),
                  pl.BlockSpec(st_block, lambda i, b: (b, j, 0, 0, 0, 0))],
        out_specs=tok,
        out_shape=jax.ShapeDtypeStruct((t, D_MODEL), F32),
        scratch_shapes=[pltpu.VMEM((RET_HEADS, tq, n), F32), pltpu.VMEM((tq, RET_WIDTH), BF16)],
        compiler_params=_params(2),
        name="ret_core",
    )(lg, q, k.reshape(nb, n, RET_QK_WIDTH), v.reshape(nb, n, RET_WIDTH), g, x, mod, gn_w, w_out, s0)


def _mlp_kernel(x_ref, mod_ref, nw_ref, wi_ref, lnw_ref, lnb_ref, ws_ref, bs_ref, wo_ref, y_ref, o_ref):
    x = x_ref[...]
    mod = mod_ref[0]
    h = _norm_mod(x, nw_ref[...], mod).astype(BF16)
    w = MLP_WIDTH
    v = _gelu_tanh(_dot(h, wi_ref[:, w:2 * w]))
    mu = jnp.mean(v, axis=-1, keepdims=True)
    vc = v - mu
    var = jnp.mean(vc * vc, axis=-1, keepdims=True)
    vn = (vc * lax.rsqrt(var + EPS) * lnw_ref[...] + lnb_ref[...]).astype(BF16)
    u = _gelu_tanh(_dot(h, wi_ref[:, 0:w]))
    ug = u * _silu(_dot(h, wi_ref[:, 2 * w:3 * w]))
    tm = x.shape[0]
    for c in range(tm // MLP_CHUNK):
        rs = slice(c * MLP_CHUNK, (c + 1) * MLP_CHUNK)
        for g in range(MLP_GROUPS):
            cs = slice(g * MLP_GROUP_DIM, (g + 1) * MLP_GROUP_DIM)
            sv = _dot(ws_ref[g], vn[rs, cs]) + bs_ref[:, cs]
            o_ref[rs, cs] = (ug[rs, cs] * sv).astype(BF16)
    y_ref[...] = x + mod[2:3] * _dot(o_ref[...], wo_ref[...])


def _mlp_layer(x, mod, nw, w_in, ln_w, ln_b, w_s, b_s_cols, w_out, j, tm, blocks_per_batch):
    t = x.shape[0]
    tok = pl.BlockSpec((tm, D_MODEL), lambda i: (i, 0))
    return pl.pallas_call(
        _mlp_kernel,
        grid=(t // tm,),
        in_specs=[tok,
                  pl.BlockSpec((1, 3, D_MODEL), lambda i: (i // blocks_per_batch, 0, 0)),
                  _const_spec((1, D_MODEL)),
                  _layer_spec((D_MODEL, 3 * MLP_WIDTH), j),
                  _const_spec((1, MLP_WIDTH)),
                  _const_spec((1, MLP_WIDTH)),
                  _layer_spec((MLP_GROUPS, MLP_CHUNK, MLP_CHUNK), j),
                  _const_spec((MLP_CHUNK, MLP_WIDTH)),
                  _layer_spec((MLP_WIDTH, D_MODEL), j)],
        out_specs=tok,
        out_shape=jax.ShapeDtypeStruct((t, D_MODEL), F32),
        scratch_shapes=[pltpu.VMEM((tm, MLP_WIDTH), BF16)],
        compiler_params=_params(1),
        name="mlp_layer",
    )(x, mod, nw, w_in, ln_w, ln_b, w_s, b_s_cols, w_out)


def kernel(x_prompt, x_sample, cache_na_k, cache_na_v, state_ret, c, c_ctx, norm_w, w_ada, b_ada,
           na_w_in, na_w_out, na_q_gain, na_k_gain, na_rpb,
           ret_w_in, ret_w_out, ret_decay_logit, ret_gn_w,
           mlp_w_in, mlp_ln_w, mlp_ln_b, mlp_w_s, mlp_b_s, mlp_w_out):
    batch, seq, _ = x_prompt.shape
    dec_batch, dec_seq, _ = x_sample.shape
    past = cache_na_k.shape[2]
    assert seq == TOKEN_BLOCK and dec_seq == 16 * GRID_W and dec_batch <= 7

    c_rows = jnp.zeros((8, D_MODEL), F32).at[:dec_batch].set(c).at[dec_batch].set(c_ctx)
    mods = _ada_all(c_rows, w_ada, b_ada).reshape(DEPTH, 8, 3, D_MODEL)

    rope_tabs = _rope_tables(dec_seq)
    na_w_in, na_w_out = na_w_in.astype(BF16), na_w_out.astype(BF16)
    ret_w_in, ret_w_out = ret_w_in.astype(BF16), ret_w_out.astype(BF16)
    mlp_w_in, mlp_w_out, mlp_w_s = mlp_w_in.astype(BF16), mlp_w_out.astype(BF16), mlp_w_s.astype(BF16)

    yp = x_prompt.reshape(batch * seq, D_MODEL)
    ys = x_sample.reshape(dec_batch * dec_seq, D_MODEL)
    bpb_s = dec_seq // TOKEN_BLOCK
    n_na = (DEPTH + 2) // N_MIXERS
    new_kv, new_s = None, []
    for i in range(DEPTH):
        kind, j = i % N_MIXERS, i // N_MIXERS
        mod_p = mods[i, dec_batch:dec_batch + 1]
        mod_s = mods[i, :dec_batch]
        nw = norm_w[i].reshape(1, D_MODEL)
        if kind == 0:
            qg = jnp.tile(na_q_gain[j], NA_HEADS).reshape(1, D_MODEL)
            kg = jnp.broadcast_to(jnp.tile(na_k_gain[j], NA_HEADS)[:, None], (D_MODEL, 128))
            yp, *new_kv = _na_prompt_layer(yp, mod_p, nw, na_w_in, na_w_out, j, qg, kg, n_na, new_kv)
            q, k, v, g = _na_proj(ys, mod_s, nw, na_w_in, j, qg, kg, bpb_s)
            kc = cache_na_k[:, j].transpose(0, 2, 3, 1).reshape(dec_batch, D_MODEL, past).astype(BF16)
            vc = cache_na_v[:, j].transpose(0, 2, 3, 1).astype(BF16)
            vc = jnp.concatenate([vc.reshape(dec_batch, NA_HEADS // 2, NA_PAIR, past),
                                  jnp.ones((dec_batch, NA_HEADS // 2, NA_PAIR, past), BF16)],
                                 axis=2).reshape(dec_batch, 2 * D_MODEL, past)
            ys = _na_latent_attn(q, k,
                                 v.reshape(dec_batch, dec_seq, 2 * D_MODEL), kc, vc,
                                 _na_bias_tables(na_rpb[j], dec_seq // GRID_W), g, ys, mod_s, na_w_out, j)
        elif kind == 1:
            lg = jax.nn.log_sigmoid(ret_decay_logit[j].astype(F32)).reshape(2 * RET_HEADS)
            gn = ret_gn_w[j].reshape(1, RET_WIDTH)
            yp, st = _ret_prompt_layer(lg, yp, mod_p, nw, ret_w_in, ret_w_out, j, gn)
            new_s.append(st)
            q, k, v, g = _ret_proj(ys, mod_s, nw, ret_w_in, j, bpb_s, rope_tabs)
            ys = _ret_core(lg, q, k, v, g, ys, mod_s, gn, ret_w_out, j, dec_seq, state_ret)
        else:
            lnw = mlp_ln_w[j].reshape(1, MLP_WIDTH)
            lnb = mlp_ln_b[j].reshape(1, MLP_WIDTH)
            b_cols = jnp.repeat(mlp_b_s[j].T, MLP_GROUP_DIM, axis=1)
            yp = _mlp_layer(yp, mod_p, nw, mlp_w_in, lnw, lnb, mlp_w_s, b_cols, mlp_w_out, j,
                            MLP_TOKEN_BLOCK, batch * seq // MLP_TOKEN_BLOCK)
            ys = _mlp_layer(ys, mod_s, nw, mlp_w_in, lnw, lnb, mlp_w_s, b_cols, mlp_w_out, j,
                            MLP_TOKEN_BLOCK, dec_seq // MLP_TOKEN_BLOCK)
    return (yp.reshape(batch, seq, D_MODEL),
            ys.reshape(dec_batch, dec_seq, D_MODEL),
            *(t.reshape(batch, n_na, NA_HEADS, NA_HEAD_DIM, seq).transpose(0, 1, 4, 2, 3) for t in new_kv),
            jnp.stack(new_s, axis=1))
```

```python
import functools
import math

import numpy as np
import jax
import jax.numpy as jnp
from jax import lax
from jax.experimental import pallas as pl
from jax.experimental.pallas import tpu as pltpu

F32 = jnp.float32
BF16 = jnp.bfloat16

D_MODEL = 1024
DEPTH = 4
N_MIXERS = 3
GRID_W = 64
EPS = 1e-6
NEG_INF = -1e30
LOG2E = 1.4426950408889634

NA_HEADS = 16
NA_HEAD_DIM = 64
NA_PAIR = 2 * NA_HEAD_DIM
NA_KH = 8
NA_KW = 16
NA_Q_ROWS = 4
NA_WIN_ROWS = 12
NA_TQ = NA_Q_ROWS * GRID_W
NA_WIN = NA_WIN_ROWS * GRID_W

RET_HEADS = 4
RET_QK_DIM = 256
RET_V_DIM = 512
RET_QK_WIDTH = RET_HEADS * RET_QK_DIM
RET_WIDTH = RET_HEADS * RET_V_DIM
ROPE_BASE = 10000.0

MLP_WIDTH = 2048
MLP_GROUPS = 8
MLP_GROUP_DIM = 256
MLP_CHUNK = 128

TOKEN_BLOCK = 256
MLP_TOKEN_BLOCK = 512
V7X_VMEM_LIMIT = 56 * 1024 * 1024

_NT = (((1,), (1,)), ((), ()))


def _dot(a, b):
    return jnp.dot(a, b, preferred_element_type=F32)


def _dot_nt(a, b):
    return lax.dot_general(a, b, _NT, preferred_element_type=F32)


def _silu(x):
    return x * jax.nn.sigmoid(x)


def _gelu_tanh(x):
    c = math.sqrt(2.0 / math.pi)
    return x * (0.5 * (1.0 + jnp.tanh(c * (x + 0.044715 * (x * x * x)))))


def _norm_mod(x, nw, mod):
    ms = jnp.mean(x * x, axis=-1, keepdims=True)
    y = x * lax.rsqrt(ms + EPS) * nw
    return y * (1.0 + mod[1:2]) + mod[0:1]


def _const_spec(shape):
    nd = len(shape)
    return pl.BlockSpec(shape, lambda *_: (0,) * nd, pipeline_mode=pl.Buffered(1))


def _side_specs(side, n_steps):
    in_specs, out_specs, out_shape = [], [], []
    for w in side:
        rows, cols = w.shape
        blk = pl.BlockSpec((rows // n_steps, cols), lambda i: (i, 0))
        in_specs.append(blk)
        out_specs.append(blk)
        out_shape.append(jax.ShapeDtypeStruct((rows, cols), BF16))
    return in_specs, out_specs, out_shape


def _side_cast(side_in, side_out):
    for src, dst in zip(side_in, side_out):
        dst[...] = src[...].astype(BF16)


def _params(n_axes, vmem=V7X_VMEM_LIMIT):
    return pltpu.CompilerParams(dimension_semantics=("arbitrary",) * n_axes,
                                vmem_limit_bytes=vmem)


def _ada_kernel(c_ref, w_ref, b_ref, o_ref):
    a = _silu(c_ref[...]).astype(BF16)
    o_ref[0] = _dot(a, w_ref[0].astype(BF16)) + b_ref[0]


def _ada_all(c_rows, w_ada, b_ada):
    n_col = 3
    return pl.pallas_call(
        _ada_kernel,
        grid=(DEPTH, n_col),
        in_specs=[
            pl.BlockSpec((8, D_MODEL), lambda i, n: (0, 0)),
            pl.BlockSpec((1, D_MODEL, D_MODEL), lambda i, n: (i, 0, n)),
            pl.BlockSpec((1, 1, D_MODEL), lambda i, n: (i, 0, n)),
        ],
        out_specs=pl.BlockSpec((1, 8, D_MODEL), lambda i, n: (i, 0, n)),
        out_shape=jax.ShapeDtypeStruct((DEPTH, 8, 3 * D_MODEL), F32),
        compiler_params=_params(2, 32 * 1024 * 1024),
        name="adaln",
    )(c_rows, w_ada, b_ada.reshape(DEPTH, 1, 3 * D_MODEL))


def _head_rms_t(t_t):
    slabs = []
    for h in range(NA_HEADS):
        blk = t_t[h * NA_HEAD_DIM:(h + 1) * NA_HEAD_DIM, :]
        ms = jnp.mean(blk * blk, axis=0, keepdims=True)
        slabs.append(blk * lax.rsqrt(ms + EPS))
    return jnp.concatenate(slabs, axis=0)


def _na_qkvg(x, mod, nw, w_ref, qg, kg_cols):
    h = _norm_mod(x, nw, mod).astype(BF16)
    w = D_MODEL
    qn = _head_rms_t(_dot(h, w_ref[:, 0:w]).T).T * qg
    q = (qn * (LOG2E * NA_HEAD_DIM ** -0.5)).astype(BF16)
    kn_t = _head_rms_t(_dot(h, w_ref[:, w:2 * w]).T)
    kn_t = kn_t * jnp.concatenate([kg_cols] * (kn_t.shape[1] // kg_cols.shape[1]), axis=1)
    v = _dot(h, w_ref[:, 2 * w:3 * w])
    g = _silu(_dot(h, w_ref[:, 3 * w:4 * w]))
    return q, kn_t, v, g


def _pair_masks():
    lane = lax.broadcasted_iota(jnp.int32, (1, NA_PAIR), 1)
    return lane < NA_HEAD_DIM


def _split_pair(q2, first):
    zero = jnp.zeros_like(q2)
    return jnp.where(first, q2, zero), jnp.where(first, zero, q2)


def _softmax_pv(s_parts, v_parts):
    m = functools.reduce(jnp.maximum, [jnp.max(s, axis=-1, keepdims=True) for s in s_parts])
    acc = None
    for s, (v, v_is_t) in zip(s_parts, v_parts):
        e = jnp.exp2(s - m).astype(BF16)
        o = _dot_nt(e, v) if v_is_t else _dot(e, v)
        acc = o if acc is None else acc + o
    return acc[:, :NA_PAIR] / acc[:, NA_PAIR:]


def _na_prompt_kernel(x_ref, mod_ref, nw_ref, w_ref, qg_ref, kg_ref, wo_ref, *rest, n_alias, n_side):
    rest = rest[n_alias:]
    y_ref, kt_ref, vt_ref = rest[n_side:n_side + 3]
    og_ref = rest[-1]
    _side_cast(rest[:n_side], rest[n_side + 3:2 * n_side + 3])
    x = x_ref[...]
    mod = mod_ref[0]
    q, kn_t, v, g = _na_qkvg(x, mod, nw_ref[...], w_ref, qg_ref[...], kg_ref[...])
    kt_ref[0, 0] = kn_t
    vt_ref[0, 0] = v.T
    kb_t, vb = kn_t.astype(BF16), v.astype(BF16)
    ones = jnp.ones((x.shape[0], NA_PAIR), BF16)
    first = _pair_masks()
    for p in range(NA_HEADS // 2):
        sl = slice(p * NA_PAIR, (p + 1) * NA_PAIR)
        k2_t = kb_t[sl, :]
        v2 = jnp.concatenate([vb[:, sl], ones], axis=1)
        outs = [_softmax_pv([_dot(qh, k2_t)], [(v2, False)]) for qh in _split_pair(q[:, sl], first)]
        og_ref[:, sl] = (jnp.where(first, outs[0], outs[1]) * g[:, sl]).astype(BF16)
    y_ref[...] = x + mod[2:3] * _dot(og_ref[...], wo_ref[...])


def _na_prompt_layer(x, mod, nw, w_in, w_out, j, q_gain, k_gain_cols, n_layers, cache_prev, side):
    t = x.shape[0]
    tm = TOKEN_BLOCK
    tok = pl.BlockSpec((tm, D_MODEL), lambda i: (i, 0))
    in_specs = [tok,
                _const_spec((1, 3, D_MODEL)),
                _const_spec((1, D_MODEL)),
                _const_spec((D_MODEL, 4 * D_MODEL)),
                _const_spec((1, D_MODEL)),
                _const_spec((D_MODEL, 128)),
                _const_spec((D_MODEL, D_MODEL))]
    args = [x, mod, nw, w_in, q_gain, k_gain_cols, w_out]
    cache = pl.BlockSpec((1, 1, D_MODEL, tm), lambda i: (i, j, 0, 0))
    cache_shape = jax.ShapeDtypeStruct((t // tm, n_layers, D_MODEL, tm), F32)
    aliases = {}
    if cache_prev is not None:
        aliases = {len(args): 1, len(args) + 1: 2}
        in_specs += [pl.BlockSpec(memory_space=pl.ANY)] * 2
        args += list(cache_prev)
    side_in, side_out, side_shape = _side_specs(side, t // tm)
    return pl.pallas_call(
        functools.partial(_na_prompt_kernel, n_alias=len(aliases), n_side=len(side)),
        grid=(t // tm,),
        in_specs=in_specs + side_in,
        out_specs=[tok, cache, cache] + side_out,
        out_shape=[jax.ShapeDtypeStruct((t, D_MODEL), F32), cache_shape, cache_shape] + side_shape,
        input_output_aliases=aliases,
        scratch_shapes=[pltpu.VMEM((tm, D_MODEL), BF16)],
        compiler_params=_params(1),
        name="na_prompt",
    )(*args, *side)


def _na_proj_kernel(x_ref, mod_ref, nw_ref, w_ref, qg_ref, kg_ref, q_out, kt_out, v_out, g_out):
    q, kn_t, v, g = _na_qkvg(x_ref[...], mod_ref[0], nw_ref[...], w_ref, qg_ref[...], kg_ref[...])
    q_out[...] = q
    kt_out[0, 0] = kn_t.astype(BF16)
    g_out[...] = g.astype(BF16)
    vb = v.astype(BF16)
    ones = jnp.ones((vb.shape[0], NA_PAIR), BF16)
    for p in range(NA_HEADS // 2):
        v_out[:, 2 * p * NA_PAIR:(2 * p + 1) * NA_PAIR] = vb[:, p * NA_PAIR:(p + 1) * NA_PAIR]
        v_out[:, (2 * p + 1) * NA_PAIR:(2 * p + 2) * NA_PAIR] = ones


def _na_proj(x, mod, nw, w_in, q_gain, k_gain_cols, blocks_per_batch):
    t = x.shape[0]
    tm = TOKEN_BLOCK
    bpb = blocks_per_batch
    tok = pl.BlockSpec((tm, D_MODEL), lambda i: (i, 0))
    tok_v = pl.BlockSpec((tm, 2 * D_MODEL), lambda i: (i, 0))
    return pl.pallas_call(
        _na_proj_kernel,
        grid=(t // tm,),
        in_specs=[tok,
                  pl.BlockSpec((1, 3, D_MODEL), lambda i: (i // bpb, 0, 0)),
                  _const_spec((1, D_MODEL)),
                  _const_spec((D_MODEL, 4 * D_MODEL)),
                  _const_spec((1, D_MODEL)),
                  _const_spec((D_MODEL, 128))],
        out_specs=[tok, pl.BlockSpec((1, 1, D_MODEL, tm), lambda i: (i // bpb, i % bpb, 0, 0)), tok_v, tok],
        out_shape=[jax.ShapeDtypeStruct((t, D_MODEL), BF16),
                   jax.ShapeDtypeStruct((t // tm // bpb, bpb, D_MODEL, tm), BF16),
                   jax.ShapeDtypeStruct((t, 2 * D_MODEL), BF16),
                   jax.ShapeDtypeStruct((t, D_MODEL), BF16)],
        compiler_params=_params(1),
        name="na_proj",
    )(x, mod, nw, w_in, q_gain, k_gain_cols)


def _na_latent_attn_kernel(q_ref, k_ref, v_ref, kc_ref, vc_ref, bias_ref, g_ref, x_ref, mod_ref,
                           wo_ref, y_ref, og_ref):
    i = pl.program_id(0)
    blk0 = i // 2
    win0 = pl.multiple_of(blk0 * NA_TQ, NA_TQ)
    first = _pair_masks()
    for p in range(NA_HEADS // 2):
        sl = slice(p * NA_PAIR, (p + 1) * NA_PAIR)
        sl2 = slice(2 * p * NA_PAIR, 2 * (p + 1) * NA_PAIR)
        kw_t = jnp.concatenate([k_ref[0, blk0 + c, sl, :] for c in range(NA_WIN // NA_TQ)], axis=1)
        vw = v_ref[0, pl.ds(win0, NA_WIN), sl2]
        kc_t, vc_t = kc_ref[0, sl, :], vc_ref[0, sl2, :]
        outs = []
        for hh, qh in enumerate(_split_pair(q_ref[:, sl], first)):
            s_loc = _dot(qh, kw_t) + bias_ref[2 * p + hh, 0].astype(F32)
            outs.append(_softmax_pv([s_loc, _dot(qh, kc_t)], [(vw, False), (vc_t, True)]))
        o2 = jnp.where(first, outs[0], outs[1])
        og_ref[:, sl] = (o2 * g_ref[:, sl].astype(F32)).astype(BF16)
    y_ref[...] = x_ref[...] + mod_ref[0][2:3] * _dot(og_ref[...], wo_ref[...])


def _na_latent_attn(q, k, v, kc, vc, bias, g, x, mod, w_out):
    nb, nq = k.shape[0], k.shape[1]
    n = nq * NA_TQ
    tok = pl.BlockSpec((NA_TQ, D_MODEL), lambda i, b: (b * nq + i, 0))
    full = pl.BlockSpec((1, nq, D_MODEL, NA_TQ), lambda i, b: (b, 0, 0, 0))
    full2 = pl.BlockSpec((1, n, 2 * D_MODEL), lambda i, b: (b, 0, 0))
    ctx = pl.BlockSpec((1, D_MODEL, kc.shape[2]), lambda i, b: (b, 0, 0))
    ctx2 = pl.BlockSpec((1, 2 * D_MODEL, kc.shape[2]), lambda i, b: (b, 0, 0))
    return pl.pallas_call(
        _na_latent_attn_kernel,
        grid=(nq, nb),
        in_specs=[tok, full, full2, ctx, ctx2,
                  pl.BlockSpec((NA_HEADS, 1, NA_TQ, NA_WIN), lambda i, b: (0, i, 0, 0)),
                  tok, tok,
                  pl.BlockSpec((1, 3, D_MODEL), lambda i, b: (b, 0, 0)),
                  _const_spec((D_MODEL, D_MODEL))],
        out_specs=tok,
        out_shape=jax.ShapeDtypeStruct((nb * n, D_MODEL), F32),
        scratch_shapes=[pltpu.VMEM((NA_TQ, D_MODEL), BF16)],
        compiler_params=_params(2),
        name="na_latent_attn",
    )(q, k, v, kc, vc, bias, g, x, mod, w_out)


def _na_window_plan(rows):
    n_blk = rows // NA_Q_ROWS
    kh = min(NA_KH, rows)
    n_chunk = NA_WIN_ROWS // 2
    a0 = np.zeros((n_blk, NA_Q_ROWS, n_chunk), np.int64)
    row_mask = np.zeros((n_blk, NA_TQ, NA_WIN), np.float32)
    for i in range(n_blk):
        win_row0 = (i // 2) * NA_Q_ROWS
        for ql in range(NA_Q_ROWS):
            qr = i * NA_Q_ROWS + ql
            rstart = min(max(qr - kh // 2, 0), rows - kh)
            for kl in range(NA_WIN_ROWS):
                kr = win_row0 + kl
                if not rstart <= kr < rstart + kh:
                    row_mask[i, ql * GRID_W:(ql + 1) * GRID_W, kl * GRID_W:(kl + 1) * GRID_W] = NEG_INF
            for m in range(n_chunk):
                a0[i, ql, m] = win_row0 + 2 * m - qr + (NA_KH - 1)
    return a0, row_mask


def _na_bias_kernel(t_ref, cm_ref, rm_ref, o_ref, *, a0):
    n_off = t_ref.shape[1]
    outside = jnp.full((GRID_W, GRID_W), NEG_INF, F32)

    def tile(a):
        return t_ref[0, a] * LOG2E if 0 <= a < n_off else outside

    pairs = {a: jnp.concatenate([tile(a), tile(a + 1)], axis=1) + cm_ref[...]
             for a in sorted(set(int(a) for a in a0.flatten()))}
    n_blk, n_ql, n_chunk = a0.shape
    for i in range(n_blk):
        for ql in range(n_ql):
            rs = slice(ql * GRID_W, (ql + 1) * GRID_W)
            for m in range(n_chunk):
                cs = slice(m * 128, (m + 1) * 128)
                o_ref[0, i, rs, cs] = (pairs[int(a0[i, ql, m])] + rm_ref[i, rs, cs].astype(F32)).astype(BF16)


def _na_bias_tables(rpb, rows):
    a0, row_mask = _na_window_plan(rows)
    n_blk = a0.shape[0]
    n_off = 2 * NA_KH - 1
    cq = np.arange(GRID_W)
    cstart = np.clip(cq - NA_KW // 2, 0, GRID_W - NA_KW)
    col_ok = (cq[None, :] >= cstart[:, None]) & (cq[None, :] < cstart[:, None] + NA_KW)
    coff = np.clip(cq[None, :] - cq[:, None], -(NA_KW - 1), NA_KW - 1) + (NA_KW - 1)
    onehot = (coff[None] == np.arange(2 * NA_KW - 1)[:, None, None]) & col_ok[None]
    t = jnp.einsum("hab,bqk->haqk", rpb, jnp.asarray(onehot, F32), precision=lax.Precision.HIGHEST)
    col_mask = np.where(np.concatenate([col_ok, col_ok], axis=1), 0.0, NEG_INF).astype(np.float32)
    return pl.pallas_call(
        functools.partial(_na_bias_kernel, a0=a0),
        grid=(NA_HEADS,),
        in_specs=[pl.BlockSpec((1, n_off, GRID_W, GRID_W), lambda h: (h, 0, 0, 0)),
                  _const_spec((GRID_W, 128)),
                  _const_spec((n_blk, NA_TQ, NA_WIN))],
        out_specs=pl.BlockSpec((1, n_blk, NA_TQ, NA_WIN), lambda h: (h, 0, 0, 0)),
        out_shape=jax.ShapeDtypeStruct((NA_HEADS, n_blk, NA_TQ, NA_WIN), BF16),
        compiler_params=_params(1, 32 * 1024 * 1024),
        name="na_bias",
    )(t, jnp.asarray(col_mask), jnp.asarray(row_mask, BF16))


def _ret_qkvg(x, mod, nw, w_ref, rope_refs):
    h = _norm_mod(x, nw, mod).astype(BF16)
    qw, vw = RET_QK_WIDTH, RET_WIDTH

    def rot(t):
        if rope_refs is None:
            return t
        cos, sin = rope_refs[0][...], rope_refs[1][...]
        parts = []
        for c in range(qw // 128):
            sl = slice(c * 128, (c + 1) * 128)
            tc = t[:, sl]
            parts.append(tc * cos[:, sl] + pltpu.roll(tc, 64, axis=1) * sin[:, sl])
        return jnp.concatenate(parts, axis=1)

    q = rot(_dot(h, w_ref[:, 0:qw])).astype(BF16)
    k = (rot(_dot(h, w_ref[:, qw:2 * qw])) * (RET_QK_DIM ** -0.5)).astype(BF16)
    v = _dot(h, w_ref[:, 2 * qw:2 * qw + vw]).astype(BF16)
    g = _silu(_dot(h, w_ref[:, 2 * qw + vw:2 * qw + 2 * vw]))
    return q, k, v, g


def _ret_decay_init(lg_ref, dec_ref, row0):
    _, tq, n = dec_ref.shape
    qi = (row0 + lax.broadcasted_iota(jnp.int32, (tq, n), 0)).astype(F32)
    diff = qi - lax.broadcasted_iota(jnp.int32, (tq, n), 1).astype(F32)
    for h in range(RET_HEADS):
        lgf, lgb = lg_ref[h], lg_ref[RET_HEADS + h]
        dec = jnp.exp(jnp.where(diff >= 0.0, lgf * diff, -lgb * diff))
        dec_ref[h] = jnp.where(diff == 0.0, 2.0, dec)


def _ret_mix(lg_ref, dec_ref, q, k, v, g, gn_ref, yg_ref, row0, s0_ref, st_ref):
    _, tq, n = dec_ref.shape
    for h in range(RET_HEADS):
        lgf, lgb = lg_ref[h], lg_ref[RET_HEADS + h]
        qh = q[:, h * RET_QK_DIM:(h + 1) * RET_QK_DIM]
        kh = k[:, h * RET_QK_DIM:(h + 1) * RET_QK_DIM]
        vh = v[:, h * RET_V_DIM:(h + 1) * RET_V_DIM]
        inner = (_dot_nt(qh, kh) * dec_ref[h]).astype(BF16)
        o = _dot(inner, vh)
        if s0_ref is not None:
            pos = (row0 + lax.broadcasted_iota(jnp.int32, (tq, 1), 0)).astype(F32)
            o = o + _dot(qh, s0_ref[0, 0, 0, h].astype(BF16)) * jnp.exp(lgf * (pos + 1.0))
            o = o + _dot(qh, s0_ref[0, 0, 1, h].astype(BF16)) * jnp.exp(lgb * (float(n) - pos))
        if st_ref is not None:
            kpos = lax.broadcasted_iota(jnp.int32, (n, RET_QK_DIM), 0).astype(F32)
            kf = kh.astype(F32)
            kdf = (kf * jnp.exp(lgf * (float(n - 1) - kpos))).T.astype(BF16)
            kdb = (kf * jnp.exp(lgb * kpos)).T.astype(BF16)
            st_ref[0, 0, h] = _dot(kdf, vh)
            st_ref[0, 1, h] = _dot(kdb, vh)
        mu = jnp.mean(o, axis=-1, keepdims=True)
        oc = o - mu
        var = jnp.mean(oc * oc, axis=-1, keepdims=True)
        vs = slice(h * RET_V_DIM, (h + 1) * RET_V_DIM)
        yn = oc * lax.rsqrt(var + EPS) * gn_ref[:, vs]
        yg_ref[:, vs] = (yn * g[:, vs].astype(F32)).astype(BF16)


def _ret_prompt_kernel(lg_ref, x_ref, mod_ref, nw_ref, w_ref, gn_ref, wo_ref, *rest, n_side):
    y_ref, st_ref = rest[n_side:n_side + 2]
    dec_ref, yg_ref = rest[-2:]
    _side_cast(rest[:n_side], rest[n_side + 2:2 * n_side + 2])

    @pl.when(pl.program_id(0) == 0)
    def _():
        _ret_decay_init(lg_ref, dec_ref, 0)

    x = x_ref[...]
    mod = mod_ref[0]
    q, k, v, g = _ret_qkvg(x, mod, nw_ref[...], w_ref, None)
    _ret_mix(lg_ref, dec_ref, q, k, v, g, gn_ref, yg_ref, 0, None, st_ref)
    y_ref[...] = x + mod[2:3] * _dot(yg_ref[...], wo_ref[...])


def _ret_prompt_layer(lg, x, mod, nw, w_in, w_out, gn_w, side):
    t = x.shape[0]
    n = TOKEN_BLOCK
    tok = pl.BlockSpec((n, D_MODEL), lambda b: (b, 0))
    st_block = (1, 2, RET_HEADS, RET_QK_DIM, RET_V_DIM)
    side_in, side_out, side_shape = _side_specs(side, t // n)
    return pl.pallas_call(
        functools.partial(_ret_prompt_kernel, n_side=len(side)),
        grid=(t // n,),
        in_specs=[pl.BlockSpec(memory_space=pltpu.SMEM),
                  tok,
                  _const_spec((1, 3, D_MODEL)),
                  _const_spec((1, D_MODEL)),
                  _const_spec((D_MODEL, 2 * RET_QK_WIDTH + 2 * RET_WIDTH)),
                  _const_spec((1, RET_WIDTH)),
                  _const_spec((RET_WIDTH, D_MODEL))] + side_in,
        out_specs=[tok, pl.BlockSpec(st_block, lambda b: (b, 0, 0, 0, 0))] + side_out,
        out_shape=[jax.ShapeDtypeStruct((t, D_MODEL), F32),
                   jax.ShapeDtypeStruct((t // n,) + st_block[1:], F32)] + side_shape,
        scratch_shapes=[pltpu.VMEM((RET_HEADS, n, n), F32), pltpu.VMEM((n, RET_WIDTH), BF16)],
        compiler_params=_params(1),
        name="ret_prompt",
    )(lg, x, mod, nw, w_in, gn_w, w_out, *side)


def _ret_proj_kernel(x_ref, mod_ref, nw_ref, w_ref, cos_ref, sin_ref, q_out, k_out, v_out, g_out):
    q, k, v, g = _ret_qkvg(x_ref[...], mod_ref[0], nw_ref[...], w_ref, (cos_ref, sin_ref))
    q_out[...] = q
    k_out[...] = k
    v_out[...] = v
    g_out[...] = g.astype(BF16)


def _ret_proj(x, mod, nw, w_in, blocks_per_batch, rope_tables):
    t = x.shape[0]
    tm = TOKEN_BLOCK
    tok = pl.BlockSpec((tm, D_MODEL), lambda i: (i, 0))
    tok2 = pl.BlockSpec((tm, RET_WIDTH), lambda i: (i, 0))
    pos = pl.BlockSpec((tm, RET_QK_WIDTH), lambda i: (i % blocks_per_batch, 0))
    return pl.pallas_call(
        _ret_proj_kernel,
        grid=(t // tm,),
        in_specs=[tok,
                  pl.BlockSpec((1, 3, D_MODEL), lambda i: (i // blocks_per_batch, 0, 0)),
                  _const_spec((1, D_MODEL)),
                  _const_spec((D_MODEL, 2 * RET_QK_WIDTH + 2 * RET_WIDTH)),
                  pos, pos],
        out_specs=[tok, tok, tok2, tok2],
        out_shape=[jax.ShapeDtypeStruct((t, RET_QK_WIDTH), BF16)] * 2
        + [jax.ShapeDtypeStruct((t, RET_WIDTH), BF16)] * 2,
        compiler_params=_params(1),
        name="ret_proj",
    )(x, mod, nw, w_in, *rope_tables)


def _rope_tables(n):
    half = RET_QK_DIM // 2
    t = np.arange(n)
    inv = ROPE_BASE ** (-jnp.arange(0, half, 2, dtype=F32) / half)
    tabs = []
    for pos in ((t // GRID_W).astype(np.float32), (t % GRID_W).astype(np.float32)):
        ang = jnp.asarray(pos)[:, None] * inv[None, :]
        tabs.append((jnp.cos(ang), jnp.sin(ang)))
    cos = jnp.concatenate([tabs[0][0], tabs[0][0], tabs[1][0], tabs[1][0]], axis=1)
    sin = jnp.concatenate([-tabs[0][1], tabs[0][1], -tabs[1][1], tabs[1][1]], axis=1)
    return jnp.tile(cos, (1, RET_HEADS)), jnp.tile(sin, (1, RET_HEADS))


def _ret_core_kernel(lg_ref, q_ref, k_ref, v_ref, g_ref, x_ref, mod_ref, gn_ref, wo_ref, s0_ref,
                     y_ref, dec_ref, yg_ref):
    row0 = pl.program_id(0) * q_ref.shape[0]

    @pl.when(pl.program_id(1) == 0)
    def _():
        _ret_decay_init(lg_ref, dec_ref, row0)

    _ret_mix(lg_ref, dec_ref, q_ref, k_ref.at[0], v_ref.at[0], g_ref, gn_ref, yg_ref, row0, s0_ref, None)
    y_ref[...] = x_ref[...] + mod_ref[0][2:3] * _dot(yg_ref[...], wo_ref[...])


def _ret_core(lg, q, k, v, g, x, mod, gn_w, w_out, j, n, s0):
    t = x.shape[0]
    nb = t // n
    tq = TOKEN_BLOCK
    nq = n // tq
    tok = pl.BlockSpec((tq, D_MODEL), lambda i, b: (b * nq + i, 0))
    tok2 = pl.BlockSpec((tq, RET_WIDTH), lambda i, b: (b * nq + i, 0))
    st_block = (1, 1, 2, RET_HEADS, RET_QK_DIM, RET_V_DIM)
    return pl.pallas_call(
        _ret_core_kernel,
        grid=(nq, nb),
        in_specs=[pl.BlockSpec(memory_space=pltpu.SMEM),
                  tok,
                  pl.BlockSpec((1, n, RET_QK_WIDTH), lambda i, b: (b, 0, 0)),
                  pl.BlockSpec((1, n, RET_WIDTH), lambda i, b: (b, 0, 0)),
                  tok2, tok,
                  pl.BlockSpec((1, 3, D_MODEL), lambda i, b: (b, 0, 0)),
                  _const_spec((1, RET_WIDTH)),
                  _const_spec((RET_WIDTH, D_MODEL)),
                  pl.BlockSpec(st_block, lambda i, b: (b, j, 0, 0, 0, 0))],
        out_specs=tok,
        out_shape=jax.ShapeDtypeStruct((t, D_MODEL), F32),
        scratch_shapes=[pltpu.VMEM((RET_HEADS, tq, n), F32), pltpu.VMEM((tq, RET_WIDTH), BF16)],
        compiler_params=_params(2),
        name="ret_core",
    )(lg, q, k.reshape(nb, n, RET_QK_WIDTH), v.reshape(nb, n, RET_WIDTH), g, x, mod, gn_w, w_out, s0)


def _mlp_kernel(x_ref, mod_ref, nw_ref, wi_ref, lnw_ref, lnb_ref, ws_ref, bs_ref, wo_ref, *rest, n_side):
    y_ref, o_ref = rest[n_side], rest[-1]
    _side_cast(rest[:n_side], rest[n_side + 1:2 * n_side + 1])
    x = x_ref[...]
    mod = mod_ref[0]
    h = _norm_mod(x, nw_ref[...], mod).astype(BF16)
    w = MLP_WIDTH
    v = _gelu_tanh(_dot(h, wi_ref[:, w:2 * w]))
    mu = jnp.mean(v, axis=-1, keepdims=True)
    vc = v - mu
    var = jnp.mean(vc * vc, axis=-1, keepdims=True)
    vn = (vc * lax.rsqrt(var + EPS) * lnw_ref[...] + lnb_ref[...]).astype(BF16)
    u = _gelu_tanh(_dot(h, wi_ref[:, 0:w]))
    ug = u * _silu(_dot(h, wi_ref[:, 2 * w:3 * w]))
    tm = x.shape[0]
    for c in range(tm // MLP_CHUNK):
        rs = slice(c * MLP_CHUNK, (c + 1) * MLP_CHUNK)
        for g in range(MLP_GROUPS):
            cs = slice(g * MLP_GROUP_DIM, (g + 1) * MLP_GROUP_DIM)
            sv = _dot(ws_ref[g], vn[rs, cs]) + bs_ref[:, cs]
            o_ref[rs, cs] = (ug[rs, cs] * sv).astype(BF16)
    y_ref[...] = x + mod[2:3] * _dot(o_ref[...], wo_ref[...])


def _mlp_layer(x, mod, nw, w_in, ln_w, ln_b, w_s, b_s_cols, w_out, tm, blocks_per_batch, side=()):
    t = x.shape[0]
    tok = pl.BlockSpec((tm, D_MODEL), lambda i: (i, 0))
    side_in, side_out, side_shape = _side_specs(side, t // tm)
    return pl.pallas_call(
        functools.partial(_mlp_kernel, n_side=len(side)),
        grid=(t // tm,),
        in_specs=[tok,
                  pl.BlockSpec((1, 3, D_MODEL), lambda i: (i // blocks_per_batch, 0, 0)),
                  _const_spec((1, D_MODEL)),
                  _const_spec((D_MODEL, 3 * MLP_WIDTH)),
                  _const_spec((1, MLP_WIDTH)),
                  _const_spec((1, MLP_WIDTH)),
                  _const_spec((MLP_GROUPS, MLP_CHUNK, MLP_CHUNK)),
                  _const_spec((MLP_CHUNK, MLP_WIDTH)),
                  _const_spec((MLP_WIDTH, D_MODEL))] + side_in,
        out_specs=[tok] + side_out,
        out_shape=[jax.ShapeDtypeStruct((t, D_MODEL), F32)] + side_shape,
        scratch_shapes=[pltpu.VMEM((tm, MLP_WIDTH), BF16)],
        compiler_params=_params(1),
        name="mlp_layer",
    )(x, mod, nw, w_in, ln_w, ln_b, w_s, b_s_cols, w_out, *side)


def kernel(x_prompt, x_sample, cache_na_k, cache_na_v, state_ret, c, c_ctx, norm_w, w_ada, b_ada,
           na_w_in, na_w_out, na_q_gain, na_k_gain, na_rpb,
           ret_w_in, ret_w_out, ret_decay_logit, ret_gn_w,
           mlp_w_in, mlp_ln_w, mlp_ln_b, mlp_w_s, mlp_b_s, mlp_w_out):
    batch, seq, _ = x_prompt.shape
    dec_batch, dec_seq, _ = x_sample.shape
    past = cache_na_k.shape[2]
    assert seq == TOKEN_BLOCK and dec_seq == 16 * GRID_W and dec_batch <= 7

    c_rows = jnp.zeros((8, D_MODEL), F32).at[:dec_batch].set(c).at[dec_batch].set(c_ctx)
    mods = _ada_all(c_rows, w_ada, b_ada).reshape(DEPTH, 8, 3, D_MODEL)

    rope_tabs = _rope_tables(dec_seq)

    def proj_weights(i):
        j = i // N_MIXERS
        return ((na_w_in[j], na_w_out[j]), (ret_w_in[j], ret_w_out[j]), (mlp_w_in[j], mlp_w_out[j]))[i % N_MIXERS]

    yp = x_prompt.reshape(batch * seq, D_MODEL)
    ys = x_sample.reshape(dec_batch * dec_seq, D_MODEL)
    bpb_s = dec_seq // TOKEN_BLOCK
    n_na = (DEPTH + 2) // N_MIXERS
    new_kv, new_s = None, []
    w_in, w_out = (w.astype(BF16) for w in proj_weights(0))
    for i in range(DEPTH):
        kind, j = i % N_MIXERS, i // N_MIXERS
        mod_p = mods[i, dec_batch:dec_batch + 1]
        mod_s = mods[i, :dec_batch]
        nw = norm_w[i].reshape(1, D_MODEL)
        side = proj_weights(i + 1) if i + 1 < DEPTH else ()
        if kind == 0:
            qg = jnp.tile(na_q_gain[j], NA_HEADS).reshape(1, D_MODEL)
            kg = jnp.broadcast_to(jnp.tile(na_k_gain[j], NA_HEADS)[:, None], (D_MODEL, 128))
            yp, k_new, v_new, *w_next = _na_prompt_layer(yp, mod_p, nw, w_in, w_out, j, qg, kg, n_na, new_kv, side)
            new_kv = (k_new, v_new)
            q, k, v, g = _na_proj(ys, mod_s, nw, w_in, qg, kg, bpb_s)
            kc = cache_na_k[:, j].transpose(0, 2, 3, 1).reshape(dec_batch, D_MODEL, past).astype(BF16)
            vc = cache_na_v[:, j].transpose(0, 2, 3, 1).astype(BF16)
            vc = jnp.concatenate([vc.reshape(dec_batch, NA_HEADS // 2, NA_PAIR, past),
                                  jnp.ones((dec_batch, NA_HEADS // 2, NA_PAIR, past), BF16)],
                                 axis=2).reshape(dec_batch, 2 * D_MODEL, past)
            ys = _na_latent_attn(q, k, v.reshape(dec_batch, dec_seq, 2 * D_MODEL), kc, vc,
                                 _na_bias_tables(na_rpb[j], dec_seq // GRID_W), g, ys, mod_s, w_out)
        elif kind == 1:
            lg = jax.nn.log_sigmoid(ret_decay_logit[j].astype(F32)).reshape(2 * RET_HEADS)
            gn = ret_gn_w[j].reshape(1, RET_WIDTH)
            yp, st, *w_next = _ret_prompt_layer(lg, yp, mod_p, nw, w_in, w_out, gn, side)
            new_s.append(st)
            q, k, v, g = _ret_proj(ys, mod_s, nw, w_in, bpb_s, rope_tabs)
            ys = _ret_core(lg, q, k, v, g, ys, mod_s, gn, w_out, j, dec_seq, state_ret)
        else:
            lnw = mlp_ln_w[j].reshape(1, MLP_WIDTH)
            lnb = mlp_ln_b[j].reshape(1, MLP_WIDTH)
            w_s = mlp_w_s[j].astype(BF16)
            b_cols = jnp.repeat(mlp_b_s[j].T, MLP_GROUP_DIM, axis=1)
            yp, *w_next = _mlp_layer(yp, mod_p, nw, w_in, lnw, lnb, w_s, b_cols, w_out,
                                     MLP_TOKEN_BLOCK, batch * seq // MLP_TOKEN_BLOCK, side)
            (ys,) = _mlp_layer(ys, mod_s, nw, w_in, lnw, lnb, w_s, b_cols, w_out,
                               MLP_TOKEN_BLOCK, dec_seq // MLP_TOKEN_BLOCK)
        if w_next:
            w_in, w_out = w_next
    return (yp.reshape(batch, seq, D_MODEL),
            ys.reshape(dec_batch, dec_seq, D_MODEL),
            *(t.reshape(batch, n_na, NA_HEADS, NA_HEAD_DIM, seq).transpose(0, 1, 4, 2, 3) for t in new_kv),
            jnp.stack(new_s, axis=1))
```

```python
import functools
import math

import numpy as np
import jax
import jax.numpy as jnp
from jax import lax
from jax.experimental import pallas as pl
from jax.experimental.pallas import tpu as pltpu

F32 = jnp.float32
BF16 = jnp.bfloat16

D_MODEL = 1024
DEPTH = 4
N_MIXERS = 3
GRID_W = 64
EPS = 1e-6
NEG_INF = -1e30
LOG2E = 1.4426950408889634

NA_HEADS = 16
NA_HEAD_DIM = 64
NA_PAIR = 2 * NA_HEAD_DIM
NA_KH = 8
NA_KW = 16
NA_Q_ROWS = 4
NA_WIN_ROWS = 12
NA_TQ = NA_Q_ROWS * GRID_W
NA_WIN = NA_WIN_ROWS * GRID_W
NA_EDGE_SLABS = 2
NA_BIAS_HEADS = 4

RET_HEADS = 4
RET_QK_DIM = 256
RET_V_DIM = 512
RET_QK_WIDTH = RET_HEADS * RET_QK_DIM
RET_WIDTH = RET_HEADS * RET_V_DIM
ROPE_BASE = 10000.0

MLP_WIDTH = 2048
MLP_GROUPS = 8
MLP_GROUP_DIM = 256
MLP_CHUNK = 128

TOKEN_BLOCK = 256
MLP_TOKEN_BLOCK = 512
V7X_VMEM_LIMIT = 56 * 1024 * 1024

_NT = (((1,), (1,)), ((), ()))


def _dot(a, b):
    return jnp.dot(a, b, preferred_element_type=F32)


def _dot_nt(a, b):
    return lax.dot_general(a, b, _NT, preferred_element_type=F32)


def _silu(x):
    return x * jax.nn.sigmoid(x)


def _gelu_tanh(x):
    c = math.sqrt(2.0 / math.pi)
    return x * (0.5 * (1.0 + jnp.tanh(c * (x + 0.044715 * (x * x * x)))))


def _norm_mod(x, nw, mod):
    ms = jnp.mean(x * x, axis=-1, keepdims=True)
    y = x * lax.rsqrt(ms + EPS) * nw
    return y * (1.0 + mod[1:2]) + mod[0:1]


def _const_spec(shape):
    nd = len(shape)
    return pl.BlockSpec(shape, lambda *_: (0,) * nd, pipeline_mode=pl.Buffered(1))


def _side_specs(side, n_steps):
    in_specs, out_specs, out_shape = [], [], []
    for w, j in side:
        _, rows, cols = w.shape
        in_specs.append(pl.BlockSpec((None, rows // n_steps, cols), lambda i, j=j: (j, i, 0)))
        out_specs.append(pl.BlockSpec((rows // n_steps, cols), lambda i: (i, 0)))
        out_shape.append(jax.ShapeDtypeStruct((rows, cols), BF16))
    return in_specs, out_specs, out_shape


def _side_cast(side_in, side_out):
    for src, dst in zip(side_in, side_out):
        dst[...] = src[...].astype(BF16)


def _params(n_axes, vmem=V7X_VMEM_LIMIT):
    return pltpu.CompilerParams(dimension_semantics=("arbitrary",) * n_axes,
                                vmem_limit_bytes=vmem)


def _ada_kernel(c_ref, w_ref, b_ref, o_ref):
    a = _silu(c_ref[...]).astype(BF16)
    for n in range(w_ref.shape[2] // D_MODEL):
        cs = slice(n * D_MODEL, (n + 1) * D_MODEL)
        o_ref[0, :, cs] = _dot(a, w_ref[0, :, cs].astype(BF16)) + b_ref[0, :, cs]


def _ada_all(c_rows, w_ada, b_ada):
    return pl.pallas_call(
        _ada_kernel,
        grid=(DEPTH,),
        in_specs=[
            pl.BlockSpec((8, D_MODEL), lambda i: (0, 0)),
            pl.BlockSpec((1, D_MODEL, 3 * D_MODEL), lambda i: (i, 0, 0)),
            pl.BlockSpec((1, 1, 3 * D_MODEL), lambda i: (i, 0, 0)),
        ],
        out_specs=pl.BlockSpec((1, 8, 3 * D_MODEL), lambda i: (i, 0, 0)),
        out_shape=jax.ShapeDtypeStruct((DEPTH, 8, 3 * D_MODEL), F32),
        compiler_params=_params(1, 40 * 1024 * 1024),
        name="adaln",
    )(c_rows, w_ada, b_ada.reshape(DEPTH, 1, 3 * D_MODEL))


def _head_rms_t(t_t):
    slabs = []
    for h in range(NA_HEADS):
        blk = t_t[h * NA_HEAD_DIM:(h + 1) * NA_HEAD_DIM, :]
        ms = jnp.mean(blk * blk, axis=0, keepdims=True)
        slabs.append(blk * lax.rsqrt(ms + EPS))
    return jnp.concatenate(slabs, axis=0)


def _na_qkvg(x, mod, nw, w_ref, qg, kg_cols):
    h = _norm_mod(x, nw, mod).astype(BF16)
    w = D_MODEL
    qn = _head_rms_t(_dot(h, w_ref[:, 0:w]).T).T * qg
    q = (qn * (LOG2E * NA_HEAD_DIM ** -0.5)).astype(BF16)
    kn_t = _head_rms_t(_dot(h, w_ref[:, w:2 * w]).T)
    kn_t = kn_t * jnp.concatenate([kg_cols] * (kn_t.shape[1] // kg_cols.shape[1]), axis=1)
    v = _dot(h, w_ref[:, 2 * w:3 * w])
    g = _silu(_dot(h, w_ref[:, 3 * w:4 * w]))
    return q, kn_t, v, g


def _pair_masks():
    lane = lax.broadcasted_iota(jnp.int32, (1, NA_PAIR), 1)
    return lane < NA_HEAD_DIM


def _split_pair(q2, first):
    zero = jnp.zeros_like(q2)
    return jnp.where(first, q2, zero), jnp.where(first, zero, q2)


def _softmax_pv(s_parts, v_parts):
    m = functools.reduce(jnp.maximum, [jnp.max(s, axis=-1, keepdims=True) for s in s_parts])
    acc = None
    for s, (v, v_is_t) in zip(s_parts, v_parts):
        e = jnp.exp2(s - m).astype(BF16)
        o = _dot_nt(e, v) if v_is_t else _dot(e, v)
        acc = o if acc is None else acc + o
    return acc[:, :NA_PAIR] / acc[:, NA_PAIR:]


def _na_prompt_kernel(x_ref, mod_ref, nw_ref, w_ref, qg_ref, kg_ref, wo_ref, *rest, n_alias, n_side):
    rest = rest[n_alias:]
    y_ref, kt_ref, vt_ref = rest[n_side:n_side + 3]
    og_ref = rest[-1]
    _side_cast(rest[:n_side], rest[n_side + 3:2 * n_side + 3])
    x = x_ref[...]
    mod = mod_ref[0]
    q, kn_t, v, g = _na_qkvg(x, mod, nw_ref[...], w_ref, qg_ref[...], kg_ref[...])
    kt_ref[0, 0] = kn_t
    vt_ref[0, 0] = v.T
    kb_t, vb = kn_t.astype(BF16), v.astype(BF16)
    ones = jnp.ones((x.shape[0], NA_PAIR), BF16)
    first = _pair_masks()
    for p in range(NA_HEADS // 2):
        sl = slice(p * NA_PAIR, (p + 1) * NA_PAIR)
        k2_t = kb_t[sl, :]
        v2 = jnp.concatenate([vb[:, sl], ones], axis=1)
        outs = [_softmax_pv([_dot(qh, k2_t)], [(v2, False)]) for qh in _split_pair(q[:, sl], first)]
        og_ref[:, sl] = (jnp.where(first, outs[0], outs[1]) * g[:, sl]).astype(BF16)
    y_ref[...] = x + mod[2:3] * _dot(og_ref[...], wo_ref[...])


def _na_prompt_layer(x, mod, nw, w_in, w_out, j, q_gain, k_gain_cols, n_layers, cache_prev, side):
    t = x.shape[0]
    tm = TOKEN_BLOCK
    tok = pl.BlockSpec((tm, D_MODEL), lambda i: (i, 0))
    in_specs = [tok,
                _const_spec((1, 3, D_MODEL)),
                _const_spec((1, D_MODEL)),
                _const_spec((D_MODEL, 4 * D_MODEL)),
                _const_spec((1, D_MODEL)),
                _const_spec((D_MODEL, 128)),
                _const_spec((D_MODEL, D_MODEL))]
    args = [x, mod, nw, w_in, q_gain, k_gain_cols, w_out]
    cache = pl.BlockSpec((1, 1, D_MODEL, tm), lambda i: (i, j, 0, 0))
    cache_shape = jax.ShapeDtypeStruct((t // tm, n_layers, D_MODEL, tm), F32)
    aliases = {}
    if cache_prev is not None:
        aliases = {len(args): 1, len(args) + 1: 2}
        in_specs += [pl.BlockSpec(memory_space=pl.ANY)] * 2
        args += list(cache_prev)
    side_in, side_out, side_shape = _side_specs(side, t // tm)
    return pl.pallas_call(
        functools.partial(_na_prompt_kernel, n_alias=len(aliases), n_side=len(side)),
        grid=(t // tm,),
        in_specs=in_specs + side_in,
        out_specs=[tok, cache, cache] + side_out,
        out_shape=[jax.ShapeDtypeStruct((t, D_MODEL), F32), cache_shape, cache_shape] + side_shape,
        input_output_aliases=aliases,
        scratch_shapes=[pltpu.VMEM((tm, D_MODEL), BF16)],
        compiler_params=_params(1),
        name="na_prompt",
    )(*args, *(w for w, _ in side))


def _na_proj_kernel(x_ref, mod_ref, nw_ref, w_ref, qg_ref, kg_ref, q_out, kt_out, v_out, g_out):
    q, kn_t, v, g = _na_qkvg(x_ref[...], mod_ref[0], nw_ref[...], w_ref, qg_ref[...], kg_ref[...])
    q_out[...] = q
    kt_out[0, 0] = kn_t.astype(BF16)
    g_out[...] = g.astype(BF16)
    vb = v.astype(BF16)
    ones = jnp.ones((vb.shape[0], NA_PAIR), BF16)
    for p in range(NA_HEADS // 2):
        v_out[:, 2 * p * NA_PAIR:(2 * p + 1) * NA_PAIR] = vb[:, p * NA_PAIR:(p + 1) * NA_PAIR]
        v_out[:, (2 * p + 1) * NA_PAIR:(2 * p + 2) * NA_PAIR] = ones


def _na_proj(x, mod, nw, w_in, q_gain, k_gain_cols, blocks_per_batch):
    t = x.shape[0]
    tm = TOKEN_BLOCK
    bpb = blocks_per_batch
    tok = pl.BlockSpec((tm, D_MODEL), lambda i: (i, 0))
    tok_v = pl.BlockSpec((tm, 2 * D_MODEL), lambda i: (i, 0))
    return pl.pallas_call(
        _na_proj_kernel,
        grid=(t // tm,),
        in_specs=[tok,
                  pl.BlockSpec((1, 3, D_MODEL), lambda i: (i // bpb, 0, 0)),
                  _const_spec((1, D_MODEL)),
                  _const_spec((D_MODEL, 4 * D_MODEL)),
                  _const_spec((1, D_MODEL)),
                  _const_spec((D_MODEL, 128))],
        out_specs=[tok, pl.BlockSpec((1, 1, D_MODEL, tm), lambda i: (i // bpb, i % bpb, 0, 0)), tok_v, tok],
        out_shape=[jax.ShapeDtypeStruct((t, D_MODEL), BF16),
                   jax.ShapeDtypeStruct((t // tm // bpb, bpb, D_MODEL, tm), BF16),
                   jax.ShapeDtypeStruct((t, 2 * D_MODEL), BF16),
                   jax.ShapeDtypeStruct((t, D_MODEL), BF16)],
        compiler_params=_params(1),
        name="na_proj",
    )(x, mod, nw, w_in, q_gain, k_gain_cols)


def _na_latent_attn_kernel(q_ref, k_ref, v_ref, kc_ref, vc_ref, bias_ref, g_ref, x_ref, mod_ref,
                           wo_ref, y_ref, og_ref):
    i = pl.program_id(0)
    last = pl.num_programs(0) - 1
    first = _pair_masks()

    def attend(n_slab, blk0):
        width = n_slab * NA_TQ
        win0 = pl.multiple_of(blk0 * NA_TQ, NA_TQ)
        for p in range(NA_HEADS // 2):
            sl = slice(p * NA_PAIR, (p + 1) * NA_PAIR)
            sl2 = slice(2 * p * NA_PAIR, 2 * (p + 1) * NA_PAIR)
            kw_t = jnp.concatenate([k_ref[0, blk0 + c, sl, :] for c in range(n_slab)], axis=1)
            vw = v_ref[0, pl.ds(win0, width), sl2]
            kc_t, vc_t = kc_ref[0, sl, :], vc_ref[0, sl2, :]
            outs = []
            for hh, qh in enumerate(_split_pair(q_ref[:, sl], first)):
                s_loc = _dot(qh, kw_t) + bias_ref[2 * p + hh, 0, :, 0:width].astype(F32)
                outs.append(_softmax_pv([s_loc, _dot(qh, kc_t)], [(vw, False), (vc_t, True)]))
            o2 = jnp.where(first, outs[0], outs[1])
            og_ref[:, sl] = (o2 * g_ref[:, sl].astype(F32)).astype(BF16)
        y_ref[...] = x_ref[...] + mod_ref[0][2:3] * _dot(og_ref[...], wo_ref[...])

    edge = jnp.logical_or(i == 0, i == last)

    @pl.when(edge)
    def _():
        attend(NA_EDGE_SLABS, jnp.where(i == 0, 0, last - 1))

    @pl.when(jnp.logical_not(edge))
    def _():
        attend(NA_WIN // NA_TQ, i - 1)


def _na_latent_attn(q, k, v, kc, vc, bias, g, x, mod, w_out):
    nb, nq = k.shape[0], k.shape[1]
    n = nq * NA_TQ
    tok = pl.BlockSpec((NA_TQ, D_MODEL), lambda i, b: (b * nq + i, 0))
    full = pl.BlockSpec((1, nq, D_MODEL, NA_TQ), lambda i, b: (b, 0, 0, 0))
    full2 = pl.BlockSpec((1, n, 2 * D_MODEL), lambda i, b: (b, 0, 0))
    ctx = pl.BlockSpec((1, D_MODEL, kc.shape[2]), lambda i, b: (b, 0, 0))
    ctx2 = pl.BlockSpec((1, 2 * D_MODEL, kc.shape[2]), lambda i, b: (b, 0, 0))
    return pl.pallas_call(
        _na_latent_attn_kernel,
        grid=(nq, nb),
        in_specs=[tok, full, full2, ctx, ctx2,
                  pl.BlockSpec((NA_HEADS, 1, NA_TQ, NA_WIN), lambda i, b: (0, i, 0, 0)),
                  tok, tok,
                  pl.BlockSpec((1, 3, D_MODEL), lambda i, b: (b, 0, 0)),
                  _const_spec((D_MODEL, D_MODEL))],
        out_specs=tok,
        out_shape=jax.ShapeDtypeStruct((nb * n, D_MODEL), F32),
        scratch_shapes=[pltpu.VMEM((NA_TQ, D_MODEL), BF16)],
        compiler_params=_params(2),
        name="na_latent_attn",
    )(q, k, v, kc, vc, bias, g, x, mod, w_out)


def _na_window_plan(rows):
    n_blk = rows // NA_Q_ROWS
    kh = min(NA_KH, rows)
    n_chunk = NA_WIN_ROWS // 2
    a0 = np.zeros((n_blk, NA_Q_ROWS, n_chunk), np.int64)
    row_mask = np.zeros((n_blk, NA_TQ, NA_WIN), np.float32)
    rstart_of = lambda qr: min(max(qr - kh // 2, 0), rows - kh)
    for i in range(n_blk):
        lo = rstart_of(i * NA_Q_ROWS)
        hi = rstart_of((i + 1) * NA_Q_ROWS - 1) + kh - 1
        win_row0 = lo // NA_Q_ROWS * NA_Q_ROWS
        n_slab = hi // NA_Q_ROWS - lo // NA_Q_ROWS + 1
        assert n_slab == (NA_EDGE_SLABS if i in (0, n_blk - 1) else NA_WIN_ROWS // NA_Q_ROWS)
        assert win_row0 == NA_Q_ROWS * (0 if i == 0 else n_blk - 2 if i == n_blk - 1 else i - 1)
        for ql in range(NA_Q_ROWS):
            qr = i * NA_Q_ROWS + ql
            rstart = rstart_of(qr)
            for kl in range(NA_WIN_ROWS):
                kr = win_row0 + kl
                if not rstart <= kr < rstart + kh:
                    row_mask[i, ql * GRID_W:(ql + 1) * GRID_W, kl * GRID_W:(kl + 1) * GRID_W] = NEG_INF
            for m in range(n_chunk):
                a0[i, ql, m] = win_row0 + 2 * m - qr + (NA_KH - 1)
    return a0, row_mask


def _na_bias_kernel(t_ref, cm_ref, rm_ref, o_ref, *, a0):
    n_off = t_ref.shape[1]
    outside = jnp.full((GRID_W, GRID_W), NEG_INF, F32)
    n_blk, n_ql, n_chunk = a0.shape
    for h in range(t_ref.shape[0]):
        def tile(a):
            return t_ref[h, a] * LOG2E if 0 <= a < n_off else outside

        pairs = {a: jnp.concatenate([tile(a), tile(a + 1)], axis=1) + cm_ref[...]
                 for a in sorted(set(int(a) for a in a0.flatten()))}
        for i in range(n_blk):
            for ql in range(n_ql):
                rs = slice(ql * GRID_W, (ql + 1) * GRID_W)
                for m in range(n_chunk):
                    cs = slice(m * 128, (m + 1) * 128)
                    o_ref[h, i, rs, cs] = (pairs[int(a0[i, ql, m])] + rm_ref[i, rs, cs].astype(F32)).astype(BF16)


def _na_bias_tables(rpb, rows):
    a0, row_mask = _na_window_plan(rows)
    n_blk = a0.shape[0]
    n_off = 2 * NA_KH - 1
    cq = np.arange(GRID_W)
    cstart = np.clip(cq - NA_KW // 2, 0, GRID_W - NA_KW)
    col_ok = (cq[None, :] >= cstart[:, None]) & (cq[None, :] < cstart[:, None] + NA_KW)
    coff = np.clip(cq[None, :] - cq[:, None], -(NA_KW - 1), NA_KW - 1) + (NA_KW - 1)
    onehot = (coff[None] == np.arange(2 * NA_KW - 1)[:, None, None]) & col_ok[None]
    t = jnp.einsum("hab,bqk->haqk", rpb, jnp.asarray(onehot, F32), precision=lax.Precision.HIGHEST)
    col_mask = np.where(np.concatenate([col_ok, col_ok], axis=1), 0.0, NEG_INF).astype(np.float32)
    return pl.pallas_call(
        functools.partial(_na_bias_kernel, a0=a0),
        grid=(NA_HEADS // NA_BIAS_HEADS,),
        in_specs=[pl.BlockSpec((NA_BIAS_HEADS, n_off, GRID_W, GRID_W), lambda h: (h, 0, 0, 0)),
                  _const_spec((GRID_W, 128)),
                  _const_spec((n_blk, NA_TQ, NA_WIN))],
        out_specs=pl.BlockSpec((NA_BIAS_HEADS, n_blk, NA_TQ, NA_WIN), lambda h: (h, 0, 0, 0)),
        out_shape=jax.ShapeDtypeStruct((NA_HEADS, n_blk, NA_TQ, NA_WIN), BF16),
        compiler_params=_params(1, 32 * 1024 * 1024),
        name="na_bias",
    )(t, jnp.asarray(col_mask), jnp.asarray(row_mask, BF16))


def _ret_qkvg(x, mod, nw, w_ref, rope_refs):
    h = _norm_mod(x, nw, mod).astype(BF16)
    qw, vw = RET_QK_WIDTH, RET_WIDTH

    def rot(t):
        if rope_refs is None:
            return t
        cos, sin = rope_refs[0][...], rope_refs[1][...]
        parts = []
        for c in range(qw // 128):
            sl = slice(c * 128, (c + 1) * 128)
            tc = t[:, sl]
            parts.append(tc * cos[:, sl] + pltpu.roll(tc, 64, axis=1) * sin[:, sl])
        return jnp.concatenate(parts, axis=1)

    q = rot(_dot(h, w_ref[:, 0:qw])).astype(BF16)
    k = (rot(_dot(h, w_ref[:, qw:2 * qw])) * (RET_QK_DIM ** -0.5)).astype(BF16)
    v = _dot(h, w_ref[:, 2 * qw:2 * qw + vw]).astype(BF16)
    g = _silu(_dot(h, w_ref[:, 2 * qw + vw:2 * qw + 2 * vw]))
    return q, k, v, g


def _ret_decay_init(lg_ref, dec_ref, row0):
    _, tq, n = dec_ref.shape
    qi = (row0 + lax.broadcasted_iota(jnp.int32, (tq, n), 0)).astype(F32)
    diff = qi - lax.broadcasted_iota(jnp.int32, (tq, n), 1).astype(F32)
    for h in range(RET_HEADS):
        lgf, lgb = lg_ref[h], lg_ref[RET_HEADS + h]
        dec = jnp.exp(jnp.where(diff >= 0.0, lgf * diff, -lgb * diff))
        dec_ref[h] = jnp.where(diff == 0.0, 2.0, dec)


def _ret_mix(lg_ref, dec_ref, q, k, v, g, gn_ref, yg_ref, row0, s0_ref, st_ref):
    _, tq, n = dec_ref.shape
    for h in range(RET_HEADS):
        lgf, lgb = lg_ref[h], lg_ref[RET_HEADS + h]
        qh = q[:, h * RET_QK_DIM:(h + 1) * RET_QK_DIM]
        kh = k[:, h * RET_QK_DIM:(h + 1) * RET_QK_DIM]
        vh = v[:, h * RET_V_DIM:(h + 1) * RET_V_DIM]
        inner = (_dot_nt(qh, kh) * dec_ref[h]).astype(BF16)
        o = _dot(inner, vh)
        if s0_ref is not None:
            pos = (row0 + lax.broadcasted_iota(jnp.int32, (tq, 1), 0)).astype(F32)
            o = o + _dot(qh, s0_ref[0, 0, 0, h].astype(BF16)) * jnp.exp(lgf * (pos + 1.0))
            o = o + _dot(qh, s0_ref[0, 0, 1, h].astype(BF16)) * jnp.exp(lgb * (float(n) - pos))
        if st_ref is not None:
            kpos = lax.broadcasted_iota(jnp.int32, (n, RET_QK_DIM), 0).astype(F32)
            kf = kh.astype(F32)
            kdf = (kf * jnp.exp(lgf * (float(n - 1) - kpos))).T.astype(BF16)
            kdb = (kf * jnp.exp(lgb * kpos)).T.astype(BF16)
            st_ref[0, 0, h] = _dot(kdf, vh)
            st_ref[0, 1, h] = _dot(kdb, vh)
        mu = jnp.mean(o, axis=-1, keepdims=True)
        oc = o - mu
        var = jnp.mean(oc * oc, axis=-1, keepdims=True)
        vs = slice(h * RET_V_DIM, (h + 1) * RET_V_DIM)
        yn = oc * lax.rsqrt(var + EPS) * gn_ref[:, vs]
        yg_ref[:, vs] = (yn * g[:, vs].astype(F32)).astype(BF16)


def _ret_prompt_kernel(lg_ref, x_ref, mod_ref, nw_ref, w_ref, gn_ref, wo_ref, *rest, n_side):
    y_ref, st_ref = rest[n_side:n_side + 2]
    dec_ref, yg_ref = rest[-2:]
    _side_cast(rest[:n_side], rest[n_side + 2:2 * n_side + 2])

    @pl.when(pl.program_id(0) == 0)
    def _():
        _ret_decay_init(lg_ref, dec_ref, 0)

    x = x_ref[...]
    mod = mod_ref[0]
    q, k, v, g = _ret_qkvg(x, mod, nw_ref[...], w_ref, None)
    _ret_mix(lg_ref, dec_ref, q, k, v, g, gn_ref, yg_ref, 0, None, st_ref)
    y_ref[...] = x + mod[2:3] * _dot(yg_ref[...], wo_ref[...])


def _ret_prompt_layer(lg, x, mod, nw, w_in, w_out, gn_w, side):
    t = x.shape[0]
    n = TOKEN_BLOCK
    tok = pl.BlockSpec((n, D_MODEL), lambda b: (b, 0))
    st_block = (1, 2, RET_HEADS, RET_QK_DIM, RET_V_DIM)
    side_in, side_out, side_shape = _side_specs(side, t // n)
    return pl.pallas_call(
        functools.partial(_ret_prompt_kernel, n_side=len(side)),
        grid=(t // n,),
        in_specs=[pl.BlockSpec(memory_space=pltpu.SMEM),
                  tok,
                  _const_spec((1, 3, D_MODEL)),
                  _const_spec((1, D_MODEL)),
                  _const_spec((D_MODEL, 2 * RET_QK_WIDTH + 2 * RET_WIDTH)),
                  _const_spec((1, RET_WIDTH)),
                  _const_spec((RET_WIDTH, D_MODEL))] + side_in,
        out_specs=[tok, pl.BlockSpec(st_block, lambda b: (b, 0, 0, 0, 0))] + side_out,
        out_shape=[jax.ShapeDtypeStruct((t, D_MODEL), F32),
                   jax.ShapeDtypeStruct((t // n,) + st_block[1:], F32)] + side_shape,
        scratch_shapes=[pltpu.VMEM((RET_HEADS, n, n), F32), pltpu.VMEM((n, RET_WIDTH), BF16)],
        compiler_params=_params(1),
        name="ret_prompt",
    )(lg, x, mod, nw, w_in, gn_w, w_out, *(w for w, _ in side))


def _ret_proj_kernel(x_ref, mod_ref, nw_ref, w_ref, cos_ref, sin_ref, q_out, k_out, v_out, g_out):
    q, k, v, g = _ret_qkvg(x_ref[...], mod_ref[0], nw_ref[...], w_ref, (cos_ref, sin_ref))
    q_out[...] = q
    k_out[...] = k
    v_out[...] = v
    g_out[...] = g.astype(BF16)


def _ret_proj(x, mod, nw, w_in, blocks_per_batch, rope_tables):
    t = x.shape[0]
    tm = TOKEN_BLOCK
    tok = pl.BlockSpec((tm, D_MODEL), lambda i: (i, 0))
    tok2 = pl.BlockSpec((tm, RET_WIDTH), lambda i: (i, 0))
    pos = pl.BlockSpec((tm, RET_QK_WIDTH), lambda i: (i % blocks_per_batch, 0))
    return pl.pallas_call(
        _ret_proj_kernel,
        grid=(t // tm,),
        in_specs=[tok,
                  pl.BlockSpec((1, 3, D_MODEL), lambda i: (i // blocks_per_batch, 0, 0)),
                  _const_spec((1, D_MODEL)),
                  _const_spec((D_MODEL, 2 * RET_QK_WIDTH + 2 * RET_WIDTH)),
                  pos, pos],
        out_specs=[tok, tok, tok2, tok2],
        out_shape=[jax.ShapeDtypeStruct((t, RET_QK_WIDTH), BF16)] * 2
        + [jax.ShapeDtypeStruct((t, RET_WIDTH), BF16)] * 2,
        compiler_params=_params(1),
        name="ret_proj",
    )(x, mod, nw, w_in, *rope_tables)


def _rope_tables(n):
    half = RET_QK_DIM // 2
    t = np.arange(n)
    inv = ROPE_BASE ** (-jnp.arange(0, half, 2, dtype=F32) / half)
    tabs = []
    for pos in ((t // GRID_W).astype(np.float32), (t % GRID_W).astype(np.float32)):
        ang = jnp.asarray(pos)[:, None] * inv[None, :]
        tabs.append((jnp.cos(ang), jnp.sin(ang)))
    cos = jnp.concatenate([tabs[0][0], tabs[0][0], tabs[1][0], tabs[1][0]], axis=1)
    sin = jnp.concatenate([-tabs[0][1], tabs[0][1], -tabs[1][1], tabs[1][1]], axis=1)
    return jnp.tile(cos, (1, RET_HEADS)), jnp.tile(sin, (1, RET_HEADS))


def _ret_core_kernel(lg_ref, q_ref, k_ref, v_ref, g_ref, x_ref, mod_ref, gn_ref, wo_ref, s0_ref,
                     y_ref, dec_ref, yg_ref):
    row0 = pl.program_id(0) * q_ref.shape[0]

    @pl.when(pl.program_id(1) == 0)
    def _():
        _ret_decay_init(lg_ref, dec_ref, row0)

    _ret_mix(lg_ref, dec_ref, q_ref, k_ref.at[0], v_ref.at[0], g_ref, gn_ref, yg_ref, row0, s0_ref, None)
    y_ref[...] = x_ref[...] + mod_ref[0][2:3] * _dot(yg_ref[...], wo_ref[...])


def _ret_core(lg, q, k, v, g, x, mod, gn_w, w_out, j, n, s0):
    t = x.shape[0]
    nb = t // n
    tq = TOKEN_BLOCK
    nq = n // tq
    tok = pl.BlockSpec((tq, D_MODEL), lambda i, b: (b * nq + i, 0))
    tok2 = pl.BlockSpec((tq, RET_WIDTH), lambda i, b: (b * nq + i, 0))
    st_block = (1, 1, 2, RET_HEADS, RET_QK_DIM, RET_V_DIM)
    return pl.pallas_call(
        _ret_core_kernel,
        grid=(nq, nb),
        in_specs=[pl.BlockSpec(memory_space=pltpu.SMEM),
                  tok,
                  pl.BlockSpec((1, n, RET_QK_WIDTH), lambda i, b: (b, 0, 0)),
                  pl.BlockSpec((1, n, RET_WIDTH), lambda i, b: (b, 0, 0)),
                  tok2, tok,
                  pl.BlockSpec((1, 3, D_MODEL), lambda i, b: (b, 0, 0)),
                  _const_spec((1, RET_WIDTH)),
                  _const_spec((RET_WIDTH, D_MODEL)),
                  pl.BlockSpec(st_block, lambda i, b: (b, j, 0, 0, 0, 0))],
        out_specs=tok,
        out_shape=jax.ShapeDtypeStruct((t, D_MODEL), F32),
        scratch_shapes=[pltpu.VMEM((RET_HEADS, tq, n), F32), pltpu.VMEM((tq, RET_WIDTH), BF16)],
        compiler_params=_params(2),
        name="ret_core",
    )(lg, q, k.reshape(nb, n, RET_QK_WIDTH), v.reshape(nb, n, RET_WIDTH), g, x, mod, gn_w, w_out, s0)


def _mlp_kernel(x_ref, mod_ref, nw_ref, wi_ref, lnw_ref, lnb_ref, ws_ref, bs_ref, wo_ref, *rest, n_side):
    y_ref, o_ref = rest[n_side], rest[-1]
    _side_cast(rest[:n_side], rest[n_side + 1:2 * n_side + 1])
    x = x_ref[...]
    mod = mod_ref[0]
    h = _norm_mod(x, nw_ref[...], mod).astype(BF16)
    w = MLP_WIDTH
    v = _gelu_tanh(_dot(h, wi_ref[:, w:2 * w]))
    mu = jnp.mean(v, axis=-1, keepdims=True)
    vc = v - mu
    var = jnp.mean(vc * vc, axis=-1, keepdims=True)
    vn = (vc * lax.rsqrt(var + EPS) * lnw_ref[...] + lnb_ref[...]).astype(BF16)
    u = _gelu_tanh(_dot(h, wi_ref[:, 0:w]))
    ug = u * _silu(_dot(h, wi_ref[:, 2 * w:3 * w]))
    tm = x.shape[0]
    for c in range(tm // MLP_CHUNK):
        rs = slice(c * MLP_CHUNK, (c + 1) * MLP_CHUNK)
        for g in range(MLP_GROUPS):
            cs = slice(g * MLP_GROUP_DIM, (g + 1) * MLP_GROUP_DIM)
            sv = _dot(ws_ref[g], vn[rs, cs]) + bs_ref[:, cs]
            o_ref[rs, cs] = (ug[rs, cs] * sv).astype(BF16)
    y_ref[...] = x + mod[2:3] * _dot(o_ref[...], wo_ref[...])


def _mlp_layer(x, mod, nw, w_in, ln_w, ln_b, w_s, b_s_cols, w_out, tm, blocks_per_batch, side=()):
    t = x.shape[0]
    tok = pl.BlockSpec((tm, D_MODEL), lambda i: (i, 0))
    side_in, side_out, side_shape = _side_specs(side, t // tm)
    return pl.pallas_call(
        functools.partial(_mlp_kernel, n_side=len(side)),
        grid=(t // tm,),
        in_specs=[tok,
                  pl.BlockSpec((1, 3, D_MODEL), lambda i: (i // blocks_per_batch, 0, 0)),
                  _const_spec((1, D_MODEL)),
                  _const_spec((D_MODEL, 3 * MLP_WIDTH)),
                  _const_spec((1, MLP_WIDTH)),
                  _const_spec((1, MLP_WIDTH)),
                  _const_spec((MLP_GROUPS, MLP_CHUNK, MLP_CHUNK)),
                  _const_spec((MLP_CHUNK, MLP_WIDTH)),
                  _const_spec((MLP_WIDTH, D_MODEL))] + side_in,
        out_specs=[tok] + side_out,
        out_shape=[jax.ShapeDtypeStruct((t, D_MODEL), F32)] + side_shape,
        scratch_shapes=[pltpu.VMEM((tm, MLP_WIDTH), BF16)],
        compiler_params=_params(1),
        name="mlp_layer",
    )(x, mod, nw, w_in, ln_w, ln_b, w_s, b_s_cols, w_out, *(w for w, _ in side))


def kernel(x_prompt, x_sample, cache_na_k, cache_na_v, state_ret, c, c_ctx, norm_w, w_ada, b_ada,
           na_w_in, na_w_out, na_q_gain, na_k_gain, na_rpb,
           ret_w_in, ret_w_out, ret_decay_logit, ret_gn_w,
           mlp_w_in, mlp_ln_w, mlp_ln_b, mlp_w_s, mlp_b_s, mlp_w_out):
    batch, seq, _ = x_prompt.shape
    dec_batch, dec_seq, _ = x_sample.shape
    past = cache_na_k.shape[2]
    assert seq == TOKEN_BLOCK and dec_seq == 16 * GRID_W and dec_batch <= 7

    c_rows = jnp.zeros((8, D_MODEL), F32).at[:dec_batch].set(c).at[dec_batch].set(c_ctx)
    mods = _ada_all(c_rows, w_ada, b_ada).reshape(DEPTH, 8, 3, D_MODEL)

    rope_tabs = _rope_tables(dec_seq)

    def proj_weights(i):
        j = i // N_MIXERS
        w_in, w_out = ((na_w_in, na_w_out), (ret_w_in, ret_w_out), (mlp_w_in, mlp_w_out))[i % N_MIXERS]
        return (w_in, j), (w_out, j)

    yp = x_prompt.reshape(batch * seq, D_MODEL)
    ys = x_sample.reshape(dec_batch * dec_seq, D_MODEL)
    bpb_s = dec_seq // TOKEN_BLOCK
    n_na = (DEPTH + 2) // N_MIXERS
    new_kv, new_s = None, []
    w_in, w_out = (w[j].astype(BF16) for w, j in proj_weights(0))
    for i in range(DEPTH):
        kind, j = i % N_MIXERS, i // N_MIXERS
        mod_p = mods[i, dec_batch:dec_batch + 1]
        mod_s = mods[i, :dec_batch]
        nw = norm_w[i].reshape(1, D_MODEL)
        side = proj_weights(i + 1) if i + 1 < DEPTH else ()
        if kind == 0:
            qg = jnp.tile(na_q_gain[j], NA_HEADS).reshape(1, D_MODEL)
            kg = jnp.broadcast_to(jnp.tile(na_k_gain[j], NA_HEADS)[:, None], (D_MODEL, 128))
            yp, k_new, v_new, *w_next = _na_prompt_layer(yp, mod_p, nw, w_in, w_out, j, qg, kg, n_na, new_kv, side)
            new_kv = (k_new, v_new)
            q, k, v, g = _na_proj(ys, mod_s, nw, w_in, qg, kg, bpb_s)
            kc = cache_na_k[:, j].transpose(0, 2, 3, 1).reshape(dec_batch, D_MODEL, past).astype(BF16)
            vc = cache_na_v[:, j].transpose(0, 2, 3, 1).astype(BF16)
            vc = jnp.concatenate([vc.reshape(dec_batch, NA_HEADS // 2, NA_PAIR, past),
                                  jnp.ones((dec_batch, NA_HEADS // 2, NA_PAIR, past), BF16)],
                                 axis=2).reshape(dec_batch, 2 * D_MODEL, past)
            ys = _na_latent_attn(q, k, v.reshape(dec_batch, dec_seq, 2 * D_MODEL), kc, vc,
                                 _na_bias_tables(na_rpb[j], dec_seq // GRID_W), g, ys, mod_s, w_out)
        elif kind == 1:
            lg = jax.nn.log_sigmoid(ret_decay_logit[j].astype(F32)).reshape(2 * RET_HEADS)
            gn = ret_gn_w[j].reshape(1, RET_WIDTH)
            yp, st, *w_next = _ret_prompt_layer(lg, yp, mod_p, nw, w_in, w_out, gn, side)
            new_s.append(st)
            q, k, v, g = _ret_proj(ys, mod_s, nw, w_in, bpb_s, rope_tabs)
            ys = _ret_core(lg, q, k, v, g, ys, mod_s, gn, w_out, j, dec_seq, state_ret)
        else:
            lnw = mlp_ln_w[j].reshape(1, MLP_WIDTH)
            lnb = mlp_ln_b[j].reshape(1, MLP_WIDTH)
            w_s = mlp_w_s[j].astype(BF16)
            b_cols = jnp.repeat(mlp_b_s[j].T, MLP_GROUP_DIM, axis=1)
            yp, *w_next = _mlp_layer(yp, mod_p, nw, w_in, lnw, lnb, w_s, b_cols, w_out,
                                     MLP_TOKEN_BLOCK, batch * seq // MLP_TOKEN_BLOCK, side)
            (ys,) = _mlp_layer(ys, mod_s, nw, w_in, lnw, lnb, w_s, b_cols, w_out,
                               MLP_TOKEN_BLOCK, dec_seq // MLP_TOKEN_BLOCK)
        if w_next:
            w_in, w_out = w_next
    return (yp.reshape(batch, seq, D_MODEL),
            ys.reshape(dec_batch, dec_seq, D_MODEL),
            *(t.reshape(batch, n_na, NA_HEADS, NA_HEAD_DIM, seq).transpose(0, 1, 4, 2, 3) for t in new_kv),
            jnp.stack(new_s, axis=1))
```

```python
import functools
import math

import numpy as np
import jax
import jax.numpy as jnp
from jax import lax
from jax.experimental import pallas as pl
from jax.experimental.pallas import tpu as pltpu

F32 = jnp.float32
BF16 = jnp.bfloat16

D_MODEL = 1024
DEPTH = 4
N_MIXERS = 3
GRID_W = 64
EPS = 1e-6
NEG_INF = -1e30
LOG2E = 1.4426950408889634

NA_HEADS = 16
NA_HEAD_DIM = 64
NA_PAIR = 2 * NA_HEAD_DIM
NA_KH = 8
NA_KW = 16
NA_Q_ROWS = 4
NA_WIN_ROWS = 12
NA_TQ = NA_Q_ROWS * GRID_W
NA_WIN = NA_WIN_ROWS * GRID_W
NA_EDGE_SLABS = 2
NA_BIAS_HEADS = 4

RET_HEADS = 4
RET_QK_DIM = 256
RET_V_DIM = 512
RET_QK_WIDTH = RET_HEADS * RET_QK_DIM
RET_WIDTH = RET_HEADS * RET_V_DIM
ROPE_BASE = 10000.0

MLP_WIDTH = 2048
MLP_GROUPS = 8
MLP_GROUP_DIM = 256
MLP_CHUNK = 128

TOKEN_BLOCK = 256
MLP_TOKEN_BLOCK = 512
V7X_VMEM_LIMIT = 56 * 1024 * 1024

_NT = (((1,), (1,)), ((), ()))


def _dot(a, b):
    return jnp.dot(a, b, preferred_element_type=F32)


def _dot_nt(a, b):
    return lax.dot_general(a, b, _NT, preferred_element_type=F32)


def _silu(x):
    return x * jax.nn.sigmoid(x)


def _gelu_tanh(x):
    c = math.sqrt(2.0 / math.pi)
    return x * (0.5 * (1.0 + jnp.tanh(c * (x + 0.044715 * (x * x * x)))))


def _norm_mod(x, nw, mod):
    ms = jnp.mean(x * x, axis=-1, keepdims=True)
    y = x * lax.rsqrt(ms + EPS) * nw
    return y * (1.0 + mod[1:2]) + mod[0:1]


def _const_spec(shape):
    nd = len(shape)
    return pl.BlockSpec(shape, lambda *_: (0,) * nd, pipeline_mode=pl.Buffered(1))


def _side_specs(side, n_steps):
    in_specs, out_specs, out_shape = [], [], []
    for w, j in side:
        _, rows, cols = w.shape
        in_specs.append(pl.BlockSpec((None, rows // n_steps, cols), lambda i, j=j: (j, i, 0)))
        out_specs.append(pl.BlockSpec((rows // n_steps, cols), lambda i: (i, 0)))
        out_shape.append(jax.ShapeDtypeStruct((rows, cols), BF16))
    return in_specs, out_specs, out_shape


def _side_cast(side_in, side_out):
    for src, dst in zip(side_in, side_out):
        dst[...] = src[...].astype(BF16)


def _params(n_axes, vmem=V7X_VMEM_LIMIT):
    return pltpu.CompilerParams(dimension_semantics=("arbitrary",) * n_axes,
                                vmem_limit_bytes=vmem)


def _ada_kernel(c_ref, w_ref, b_ref, o_ref):
    a = _silu(c_ref[...]).astype(BF16)
    for n in range(w_ref.shape[2] // D_MODEL):
        cs = slice(n * D_MODEL, (n + 1) * D_MODEL)
        o_ref[0, :, cs] = _dot(a, w_ref[0, :, cs].astype(BF16)) + b_ref[0, :, cs]


def _ada_all(c_rows, w_ada, b_ada):
    return pl.pallas_call(
        _ada_kernel,
        grid=(DEPTH,),
        in_specs=[
            pl.BlockSpec((8, D_MODEL), lambda i: (0, 0)),
            pl.BlockSpec((1, D_MODEL, 3 * D_MODEL), lambda i: (i, 0, 0)),
            pl.BlockSpec((1, 1, 3 * D_MODEL), lambda i: (i, 0, 0)),
        ],
        out_specs=pl.BlockSpec((1, 8, 3 * D_MODEL), lambda i: (i, 0, 0)),
        out_shape=jax.ShapeDtypeStruct((DEPTH, 8, 3 * D_MODEL), F32),
        compiler_params=_params(1, 40 * 1024 * 1024),
        name="adaln",
    )(c_rows, w_ada, b_ada.reshape(DEPTH, 1, 3 * D_MODEL))


def _head_rms_t(t_t):
    slabs = []
    for h in range(NA_HEADS):
        blk = t_t[h * NA_HEAD_DIM:(h + 1) * NA_HEAD_DIM, :]
        ms = jnp.mean(blk * blk, axis=0, keepdims=True)
        slabs.append(blk * lax.rsqrt(ms + EPS))
    return jnp.concatenate(slabs, axis=0)


def _na_qkvg(x, mod, nw, w_ref, qg, kg_cols):
    h = _norm_mod(x, nw, mod).astype(BF16)
    w = D_MODEL
    qn = _head_rms_t(_dot(h, w_ref[:, 0:w]).T).T * qg
    q = (qn * (LOG2E * NA_HEAD_DIM ** -0.5)).astype(BF16)
    kn_t = _head_rms_t(_dot(h, w_ref[:, w:2 * w]).T)
    kn_t = kn_t * jnp.concatenate([kg_cols] * (kn_t.shape[1] // kg_cols.shape[1]), axis=1)
    v = _dot(h, w_ref[:, 2 * w:3 * w])
    g = _silu(_dot(h, w_ref[:, 3 * w:4 * w]))
    return q, kn_t, v, g


def _pair_masks():
    lane = lax.broadcasted_iota(jnp.int32, (1, NA_PAIR), 1)
    return lane < NA_HEAD_DIM


def _split_pair(q2, first):
    zero = jnp.zeros_like(q2)
    return jnp.where(first, q2, zero), jnp.where(first, zero, q2)


def _softmax_pv(s_parts, v_parts):
    m = functools.reduce(jnp.maximum, [jnp.max(s, axis=-1, keepdims=True) for s in s_parts])
    acc = None
    for s, (v, v_is_t) in zip(s_parts, v_parts):
        e = jnp.exp2(s - m).astype(BF16)
        o = _dot_nt(e, v) if v_is_t else _dot(e, v)
        acc = o if acc is None else acc + o
    return acc[:, :NA_PAIR] / acc[:, NA_PAIR:]


def _na_prompt_kernel(x_ref, mod_ref, nw_ref, w_ref, qg_ref, kg_ref, wo_ref, *rest, n_alias, n_side):
    rest = rest[n_alias:]
    y_ref, kt_ref, vt_ref = rest[n_side:n_side + 3]
    og_ref = rest[-1]
    _side_cast(rest[:n_side], rest[n_side + 3:2 * n_side + 3])
    x = x_ref[...]
    mod = mod_ref[0]
    q, kn_t, v, g = _na_qkvg(x, mod, nw_ref[...], w_ref, qg_ref[...], kg_ref[...])
    kt_ref[0, 0] = kn_t
    vt_ref[0, 0] = v.T
    kb_t, vb = kn_t.astype(BF16), v.astype(BF16)
    ones = jnp.ones((x.shape[0], NA_PAIR), BF16)
    first = _pair_masks()
    for p in range(NA_HEADS // 2):
        sl = slice(p * NA_PAIR, (p + 1) * NA_PAIR)
        k2_t = kb_t[sl, :]
        v2 = jnp.concatenate([vb[:, sl], ones], axis=1)
        outs = [_softmax_pv([_dot(qh, k2_t)], [(v2, False)]) for qh in _split_pair(q[:, sl], first)]
        og_ref[:, sl] = (jnp.where(first, outs[0], outs[1]) * g[:, sl]).astype(BF16)
    y_ref[...] = x + mod[2:3] * _dot(og_ref[...], wo_ref[...])


def _na_prompt_layer(x, mod, nw, w_in, w_out, j, q_gain, k_gain_cols, n_layers, cache_prev, side):
    t = x.shape[0]
    tm = TOKEN_BLOCK
    tok = pl.BlockSpec((tm, D_MODEL), lambda i: (i, 0))
    in_specs = [tok,
                _const_spec((1, 3, D_MODEL)),
                _const_spec((1, D_MODEL)),
                _const_spec((D_MODEL, 4 * D_MODEL)),
                _const_spec((1, D_MODEL)),
                _const_spec((D_MODEL, 128)),
                _const_spec((D_MODEL, D_MODEL))]
    args = [x, mod, nw, w_in, q_gain, k_gain_cols, w_out]
    cache = pl.BlockSpec((1, 1, D_MODEL, tm), lambda i: (i, j, 0, 0))
    cache_shape = jax.ShapeDtypeStruct((t // tm, n_layers, D_MODEL, tm), F32)
    aliases = {}
    if cache_prev is not None:
        aliases = {len(args): 1, len(args) + 1: 2}
        in_specs += [pl.BlockSpec(memory_space=pl.ANY)] * 2
        args += list(cache_prev)
    side_in, side_out, side_shape = _side_specs(side, t // tm)
    return pl.pallas_call(
        functools.partial(_na_prompt_kernel, n_alias=len(aliases), n_side=len(side)),
        grid=(t // tm,),
        in_specs=in_specs + side_in,
        out_specs=[tok, cache, cache] + side_out,
        out_shape=[jax.ShapeDtypeStruct((t, D_MODEL), F32), cache_shape, cache_shape] + side_shape,
        input_output_aliases=aliases,
        scratch_shapes=[pltpu.VMEM((tm, D_MODEL), BF16)],
        compiler_params=_params(1),
        name="na_prompt",
    )(*args, *(w for w, _ in side))


def _na_proj_kernel(x_ref, mod_ref, nw_ref, w_ref, qg_ref, kg_ref, q_out, kt_out, v_out, g_out):
    q, kn_t, v, g = _na_qkvg(x_ref[...], mod_ref[0], nw_ref[...], w_ref, qg_ref[...], kg_ref[...])
    q_out[...] = q
    kt_out[0, 0] = kn_t.astype(BF16)
    g_out[...] = g.astype(BF16)
    vb = v.astype(BF16)
    ones = jnp.ones((vb.shape[0], NA_PAIR), BF16)
    for p in range(NA_HEADS // 2):
        v_out[:, 2 * p * NA_PAIR:(2 * p + 1) * NA_PAIR] = vb[:, p * NA_PAIR:(p + 1) * NA_PAIR]
        v_out[:, (2 * p + 1) * NA_PAIR:(2 * p + 2) * NA_PAIR] = ones


def _na_proj(x, mod, nw, w_in, q_gain, k_gain_cols, blocks_per_batch):
    t = x.shape[0]
    tm = TOKEN_BLOCK
    bpb = blocks_per_batch
    tok = pl.BlockSpec((tm, D_MODEL), lambda i: (i, 0))
    tok_v = pl.BlockSpec((tm, 2 * D_MODEL), lambda i: (i, 0))
    return pl.pallas_call(
        _na_proj_kernel,
        grid=(t // tm,),
        in_specs=[tok,
                  pl.BlockSpec((1, 3, D_MODEL), lambda i: (i // bpb, 0, 0)),
                  _const_spec((1, D_MODEL)),
                  _const_spec((D_MODEL, 4 * D_MODEL)),
                  _const_spec((1, D_MODEL)),
                  _const_spec((D_MODEL, 128))],
        out_specs=[tok, pl.BlockSpec((1, 1, D_MODEL, tm), lambda i: (i // bpb, i % bpb, 0, 0)), tok_v, tok],
        out_shape=[jax.ShapeDtypeStruct((t, D_MODEL), BF16),
                   jax.ShapeDtypeStruct((t // tm // bpb, bpb, D_MODEL, tm), BF16),
                   jax.ShapeDtypeStruct((t, 2 * D_MODEL), BF16),
                   jax.ShapeDtypeStruct((t, D_MODEL), BF16)],
        compiler_params=_params(1),
        name="na_proj",
    )(x, mod, nw, w_in, q_gain, k_gain_cols)


def _na_latent_attn_kernel(q_ref, k_ref, v_ref, kc_ref, vc_ref, bias_ref, g_ref, x_ref, mod_ref,
                           wo_ref, y_ref, og_ref):
    i = pl.program_id(0)
    last = pl.num_programs(0) - 1
    first = _pair_masks()

    def attend(n_slab, blk0):
        width = n_slab * NA_TQ
        win0 = pl.multiple_of(blk0 * NA_TQ, NA_TQ)
        for p in range(NA_HEADS // 2):
            sl = slice(p * NA_PAIR, (p + 1) * NA_PAIR)
            sl2 = slice(2 * p * NA_PAIR, 2 * (p + 1) * NA_PAIR)
            kw_t = jnp.concatenate([k_ref[0, blk0 + c, sl, :] for c in range(n_slab)], axis=1)
            vw = v_ref[0, pl.ds(win0, width), sl2]
            kc_t, vc_t = kc_ref[0, sl, :], vc_ref[0, sl2, :]
            outs = []
            for hh, qh in enumerate(_split_pair(q_ref[:, sl], first)):
                s_loc = _dot(qh, kw_t) + bias_ref[2 * p + hh, 0, :, 0:width].astype(F32)
                outs.append(_softmax_pv([s_loc, _dot(qh, kc_t)], [(vw, False), (vc_t, True)]))
            o2 = jnp.where(first, outs[0], outs[1])
            og_ref[:, sl] = (o2 * g_ref[:, sl].astype(F32)).astype(BF16)
        y_ref[...] = x_ref[...] + mod_ref[0][2:3] * _dot(og_ref[...], wo_ref[...])

    edge = jnp.logical_or(i == 0, i == last)

    @pl.when(edge)
    def _():
        attend(NA_EDGE_SLABS, jnp.where(i == 0, 0, last - 1))

    @pl.when(jnp.logical_not(edge))
    def _():
        attend(NA_WIN // NA_TQ, i - 1)


def _na_latent_attn(q, k, v, kc, vc, bias, g, x, mod, w_out):
    nb, nq = k.shape[0], k.shape[1]
    n = nq * NA_TQ
    tok = pl.BlockSpec((NA_TQ, D_MODEL), lambda i, b: (b * nq + i, 0))
    full = pl.BlockSpec((1, nq, D_MODEL, NA_TQ), lambda i, b: (b, 0, 0, 0))
    full2 = pl.BlockSpec((1, n, 2 * D_MODEL), lambda i, b: (b, 0, 0))
    ctx = pl.BlockSpec((1, D_MODEL, kc.shape[2]), lambda i, b: (b, 0, 0))
    ctx2 = pl.BlockSpec((1, 2 * D_MODEL, kc.shape[2]), lambda i, b: (b, 0, 0))
    return pl.pallas_call(
        _na_latent_attn_kernel,
        grid=(nq, nb),
        in_specs=[tok, full, full2, ctx, ctx2,
                  pl.BlockSpec((NA_HEADS, 1, NA_TQ, NA_WIN), lambda i, b: (0, i, 0, 0)),
                  tok, tok,
                  pl.BlockSpec((1, 3, D_MODEL), lambda i, b: (b, 0, 0)),
                  _const_spec((D_MODEL, D_MODEL))],
        out_specs=tok,
        out_shape=jax.ShapeDtypeStruct((nb * n, D_MODEL), F32),
        scratch_shapes=[pltpu.VMEM((NA_TQ, D_MODEL), BF16)],
        compiler_params=_params(2),
        name="na_latent_attn",
    )(q, k, v, kc, vc, bias, g, x, mod, w_out)


def _na_window_plan(rows):
    n_blk = rows // NA_Q_ROWS
    kh = min(NA_KH, rows)
    n_chunk = NA_WIN_ROWS // 2
    a0 = np.zeros((n_blk, NA_Q_ROWS, n_chunk), np.int64)
    row_mask = np.zeros((n_blk, NA_TQ, NA_WIN), np.float32)
    rstart_of = lambda qr: min(max(qr - kh // 2, 0), rows - kh)
    for i in range(n_blk):
        lo = rstart_of(i * NA_Q_ROWS)
        hi = rstart_of((i + 1) * NA_Q_ROWS - 1) + kh - 1
        win_row0 = lo // NA_Q_ROWS * NA_Q_ROWS
        n_slab = hi // NA_Q_ROWS - lo // NA_Q_ROWS + 1
        assert n_slab == (NA_EDGE_SLABS if i in (0, n_blk - 1) else NA_WIN_ROWS // NA_Q_ROWS)
        assert win_row0 == NA_Q_ROWS * (0 if i == 0 else n_blk - 2 if i == n_blk - 1 else i - 1)
        for ql in range(NA_Q_ROWS):
            qr = i * NA_Q_ROWS + ql
            rstart = rstart_of(qr)
            for kl in range(NA_WIN_ROWS):
                kr = win_row0 + kl
                if not rstart <= kr < rstart + kh:
                    row_mask[i, ql * GRID_W:(ql + 1) * GRID_W, kl * GRID_W:(kl + 1) * GRID_W] = NEG_INF
            for m in range(n_chunk):
                a0[i, ql, m] = win_row0 + 2 * m - qr + (NA_KH - 1)
    return a0, row_mask


def _na_bias_kernel(t_ref, cm_ref, rm_ref, o_ref, *, a0):
    n_off = t_ref.shape[1]
    outside = jnp.full((GRID_W, GRID_W), NEG_INF, F32)
    n_blk, n_ql, n_chunk = a0.shape
    for h in range(t_ref.shape[0]):
        def tile(a):
            return t_ref[h, a] * LOG2E if 0 <= a < n_off else outside

        pairs = {a: jnp.concatenate([tile(a), tile(a + 1)], axis=1) + cm_ref[...]
                 for a in sorted(set(int(a) for a in a0.flatten()))}
        for i in range(n_blk):
            for ql in range(n_ql):
                rs = slice(ql * GRID_W, (ql + 1) * GRID_W)
                for m in range(n_chunk):
                    cs = slice(m * 128, (m + 1) * 128)
                    o_ref[h, i, rs, cs] = (pairs[int(a0[i, ql, m])] + rm_ref[i, rs, cs].astype(F32)).astype(BF16)


def _na_column_window():
    cq = np.arange(GRID_W)
    cstart = np.clip(cq - NA_KW // 2, 0, GRID_W - NA_KW)
    col_ok = (cq[None, :] >= cstart[:, None]) & (cq[None, :] < cstart[:, None] + NA_KW)
    coff = np.clip(cq[None, :] - cq[:, None], -(NA_KW - 1), NA_KW - 1) + (NA_KW - 1)
    return col_ok, coff


def _na_bias_tiles(rpb_all):
    col_ok, coff = _na_column_window()
    onehot = (coff[None] == np.arange(2 * NA_KW - 1)[:, None, None]) & col_ok[None]
    return jnp.einsum("lhab,bqk->lhaqk", rpb_all, jnp.asarray(onehot, F32), precision=lax.Precision.HIGHEST)


def _na_bias_tables(tiles, j, rows):
    a0, row_mask = _na_window_plan(rows)
    n_blk = a0.shape[0]
    n_off = 2 * NA_KH - 1
    col_ok, _ = _na_column_window()
    col_mask = np.where(np.concatenate([col_ok, col_ok], axis=1), 0.0, NEG_INF).astype(np.float32)
    return pl.pallas_call(
        functools.partial(_na_bias_kernel, a0=a0),
        grid=(NA_HEADS // NA_BIAS_HEADS,),
        in_specs=[pl.BlockSpec((None, NA_BIAS_HEADS, n_off, GRID_W, GRID_W), lambda h: (j, h, 0, 0, 0)),
                  _const_spec((GRID_W, 128)),
                  _const_spec((n_blk, NA_TQ, NA_WIN))],
        out_specs=pl.BlockSpec((NA_BIAS_HEADS, n_blk, NA_TQ, NA_WIN), lambda h: (h, 0, 0, 0)),
        out_shape=jax.ShapeDtypeStruct((NA_HEADS, n_blk, NA_TQ, NA_WIN), BF16),
        compiler_params=_params(1, 32 * 1024 * 1024),
        name="na_bias",
    )(tiles, jnp.asarray(col_mask), jnp.asarray(row_mask, BF16))


def _ret_qkvg(x, mod, nw, w_ref, rope_refs):
    h = _norm_mod(x, nw, mod).astype(BF16)
    qw, vw = RET_QK_WIDTH, RET_WIDTH

    def rot(t):
        if rope_refs is None:
            return t
        cos, sin = rope_refs[0][...], rope_refs[1][...]
        parts = []
        for c in range(qw // 128):
            tc = t[:, c * 128:(c + 1) * 128]
            hs = slice(c * 128 % RET_QK_DIM, c * 128 % RET_QK_DIM + 128)
            parts.append(tc * cos[:, hs] + pltpu.roll(tc, 64, axis=1) * sin[:, hs])
        return jnp.concatenate(parts, axis=1)

    q = rot(_dot(h, w_ref[:, 0:qw])).astype(BF16)
    k = (rot(_dot(h, w_ref[:, qw:2 * qw])) * (RET_QK_DIM ** -0.5)).astype(BF16)
    v = _dot(h, w_ref[:, 2 * qw:2 * qw + vw]).astype(BF16)
    g = _silu(_dot(h, w_ref[:, 2 * qw + vw:2 * qw + 2 * vw]))
    return q, k, v, g


def _ret_decay_init(lg_ref, dec_ref, row0):
    _, tq, n = dec_ref.shape
    qi = (row0 + lax.broadcasted_iota(jnp.int32, (tq, n), 0)).astype(F32)
    diff = qi - lax.broadcasted_iota(jnp.int32, (tq, n), 1).astype(F32)
    for h in range(RET_HEADS):
        lgf, lgb = lg_ref[h], lg_ref[RET_HEADS + h]
        dec = jnp.exp(jnp.where(diff >= 0.0, lgf * diff, -lgb * diff))
        dec_ref[h] = jnp.where(diff == 0.0, 2.0, dec).astype(dec_ref.dtype)


def _ret_mix(lg_ref, dec_ref, q, k, v, g, gn_ref, yg_ref, row0, s0_ref, st_ref):
    _, tq, n = dec_ref.shape
    for h in range(RET_HEADS):
        lgf, lgb = lg_ref[h], lg_ref[RET_HEADS + h]
        qh = q[:, h * RET_QK_DIM:(h + 1) * RET_QK_DIM]
        kh = k[:, h * RET_QK_DIM:(h + 1) * RET_QK_DIM]
        vh = v[:, h * RET_V_DIM:(h + 1) * RET_V_DIM]
        inner = (_dot_nt(qh, kh) * dec_ref[h]).astype(BF16)
        o = _dot(inner, vh)
        if s0_ref is not None:
            pos = (row0 + lax.broadcasted_iota(jnp.int32, (tq, 1), 0)).astype(F32)
            o = o + _dot(qh, s0_ref[0, 0, 0, h].astype(BF16)) * jnp.exp(lgf * (pos + 1.0))
            o = o + _dot(qh, s0_ref[0, 0, 1, h].astype(BF16)) * jnp.exp(lgb * (float(n) - pos))
        if st_ref is not None:
            kpos = lax.broadcasted_iota(jnp.int32, (n, RET_QK_DIM), 0).astype(F32)
            kf = kh.astype(F32)
            kdf = (kf * jnp.exp(lgf * (float(n - 1) - kpos))).T.astype(BF16)
            kdb = (kf * jnp.exp(lgb * kpos)).T.astype(BF16)
            st_ref[0, 0, h] = _dot(kdf, vh)
            st_ref[0, 1, h] = _dot(kdb, vh)
        mu = jnp.mean(o, axis=-1, keepdims=True)
        oc = o - mu
        var = jnp.mean(oc * oc, axis=-1, keepdims=True)
        vs = slice(h * RET_V_DIM, (h + 1) * RET_V_DIM)
        yn = oc * lax.rsqrt(var + EPS) * gn_ref[:, vs]
        yg_ref[:, vs] = (yn * g[:, vs].astype(F32)).astype(BF16)


def _ret_prompt_kernel(lg_ref, x_ref, mod_ref, nw_ref, w_ref, gn_ref, wo_ref, *rest, n_side):
    y_ref, st_ref = rest[n_side:n_side + 2]
    dec_ref, yg_ref = rest[-2:]
    _side_cast(rest[:n_side], rest[n_side + 2:2 * n_side + 2])

    @pl.when(pl.program_id(0) == 0)
    def _():
        _ret_decay_init(lg_ref, dec_ref, 0)

    x = x_ref[...]
    mod = mod_ref[0]
    q, k, v, g = _ret_qkvg(x, mod, nw_ref[...], w_ref, None)
    _ret_mix(lg_ref, dec_ref, q, k, v, g, gn_ref, yg_ref, 0, None, st_ref)
    y_ref[...] = x + mod[2:3] * _dot(yg_ref[...], wo_ref[...])


def _ret_prompt_layer(lg, x, mod, nw, w_in, w_out, gn_w, side):
    t = x.shape[0]
    n = TOKEN_BLOCK
    tok = pl.BlockSpec((n, D_MODEL), lambda b: (b, 0))
    st_block = (1, 2, RET_HEADS, RET_QK_DIM, RET_V_DIM)
    side_in, side_out, side_shape = _side_specs(side, t // n)
    return pl.pallas_call(
        functools.partial(_ret_prompt_kernel, n_side=len(side)),
        grid=(t // n,),
        in_specs=[pl.BlockSpec(memory_space=pltpu.SMEM),
                  tok,
                  _const_spec((1, 3, D_MODEL)),
                  _const_spec((1, D_MODEL)),
                  _const_spec((D_MODEL, 2 * RET_QK_WIDTH + 2 * RET_WIDTH)),
                  _const_spec((1, RET_WIDTH)),
                  _const_spec((RET_WIDTH, D_MODEL))] + side_in,
        out_specs=[tok, pl.BlockSpec(st_block, lambda b: (b, 0, 0, 0, 0))] + side_out,
        out_shape=[jax.ShapeDtypeStruct((t, D_MODEL), F32),
                   jax.ShapeDtypeStruct((t // n,) + st_block[1:], F32)] + side_shape,
        scratch_shapes=[pltpu.VMEM((RET_HEADS, n, n), F32), pltpu.VMEM((n, RET_WIDTH), BF16)],
        compiler_params=_params(1),
        name="ret_prompt",
    )(lg, x, mod, nw, w_in, gn_w, w_out, *(w for w, _ in side))


def _ret_proj_kernel(x_ref, mod_ref, nw_ref, w_ref, cos_ref, sin_ref, q_out, k_out, v_out, g_out):
    q, k, v, g = _ret_qkvg(x_ref[...], mod_ref[0], nw_ref[...], w_ref, (cos_ref, sin_ref))
    q_out[...] = q
    k_out[...] = k
    v_out[...] = v
    g_out[...] = g.astype(BF16)


def _ret_proj(x, mod, nw, w_in, blocks_per_batch, rope_tables):
    t = x.shape[0]
    tm = TOKEN_BLOCK
    tok = pl.BlockSpec((tm, D_MODEL), lambda i: (i, 0))
    tok2 = pl.BlockSpec((tm, RET_WIDTH), lambda i: (i, 0))
    pos = pl.BlockSpec((tm, RET_QK_DIM), lambda i: (i % blocks_per_batch, 0))
    return pl.pallas_call(
        _ret_proj_kernel,
        grid=(t // tm,),
        in_specs=[tok,
                  pl.BlockSpec((1, 3, D_MODEL), lambda i: (i // blocks_per_batch, 0, 0)),
                  _const_spec((1, D_MODEL)),
                  _const_spec((D_MODEL, 2 * RET_QK_WIDTH + 2 * RET_WIDTH)),
                  pos, pos],
        out_specs=[tok, tok, tok2, tok2],
        out_shape=[jax.ShapeDtypeStruct((t, RET_QK_WIDTH), BF16)] * 2
        + [jax.ShapeDtypeStruct((t, RET_WIDTH), BF16)] * 2,
        compiler_params=_params(1),
        name="ret_proj",
    )(x, mod, nw, w_in, *rope_tables)


def _rope_tables(n):
    half = RET_QK_DIM // 2
    t = np.arange(n)
    inv = ROPE_BASE ** (-np.arange(0, half, 2, dtype=np.float64) / half)
    tabs = []
    for pos in (t // GRID_W, t % GRID_W):
        ang = pos[:, None] * inv[None, :]
        tabs.append((np.cos(ang), np.sin(ang)))
    cos = np.concatenate([tabs[0][0], tabs[0][0], tabs[1][0], tabs[1][0]], axis=1)
    sin = np.concatenate([-tabs[0][1], tabs[0][1], -tabs[1][1], tabs[1][1]], axis=1)
    return jnp.asarray(cos, F32), jnp.asarray(sin, F32)


def _ret_core_kernel(lg_ref, q_ref, k_ref, v_ref, g_ref, x_ref, mod_ref, gn_ref, wo_ref, s0_ref,
                     y_ref, dec_ref, yg_ref):
    t = pl.program_id(1)
    row0 = t * q_ref.shape[0]
    dec_t = dec_ref.at[t]

    @pl.when(pl.program_id(0) == 0)
    def _():
        _ret_decay_init(lg_ref, dec_t, row0)

    _ret_mix(lg_ref, dec_t, q_ref, k_ref.at[0], v_ref.at[0], g_ref, gn_ref, yg_ref, row0, s0_ref, None)
    y_ref[...] = x_ref[...] + mod_ref[0][2:3] * _dot(yg_ref[...], wo_ref[...])


def _ret_core(lg, q, k, v, g, x, mod, gn_w, w_out, j, n, s0):
    t = x.shape[0]
    nb = t // n
    tq = TOKEN_BLOCK
    nq = n // tq
    tok = pl.BlockSpec((tq, D_MODEL), lambda b, i: (b * nq + i, 0))
    tok2 = pl.BlockSpec((tq, RET_WIDTH), lambda b, i: (b * nq + i, 0))
    st_block = (1, 1, 2, RET_HEADS, RET_QK_DIM, RET_V_DIM)
    return pl.pallas_call(
        _ret_core_kernel,
        grid=(nb, nq),
        in_specs=[pl.BlockSpec(memory_space=pltpu.SMEM),
                  tok,
                  pl.BlockSpec((1, n, RET_QK_WIDTH), lambda b, i: (b, 0, 0)),
                  pl.BlockSpec((1, n, RET_WIDTH), lambda b, i: (b, 0, 0)),
                  tok2, tok,
                  pl.BlockSpec((1, 3, D_MODEL), lambda b, i: (b, 0, 0)),
                  _const_spec((1, RET_WIDTH)),
                  _const_spec((RET_WIDTH, D_MODEL)),
                  pl.BlockSpec(st_block, lambda b, i: (b, j, 0, 0, 0, 0))],
        out_specs=tok,
        out_shape=jax.ShapeDtypeStruct((t, D_MODEL), F32),
        scratch_shapes=[pltpu.VMEM((nq, RET_HEADS, tq, n), BF16), pltpu.VMEM((tq, RET_WIDTH), BF16)],
        compiler_params=_params(2),
        name="ret_core",
    )(lg, q, k.reshape(nb, n, RET_QK_WIDTH), v.reshape(nb, n, RET_WIDTH), g, x, mod, gn_w, w_out, s0)


def _mlp_kernel(x_ref, mod_ref, nw_ref, wi_ref, lnw_ref, lnb_ref, ws_ref, bs_ref, wo_ref, *rest, n_side):
    y_ref, o_ref = rest[n_side], rest[-1]
    _side_cast(rest[:n_side], rest[n_side + 1:2 * n_side + 1])
    x = x_ref[...]
    mod = mod_ref[0]
    h = _norm_mod(x, nw_ref[...], mod).astype(BF16)
    w = MLP_WIDTH
    v = _gelu_tanh(_dot(h, wi_ref[:, w:2 * w]))
    mu = jnp.mean(v, axis=-1, keepdims=True)
    vc = v - mu
    var = jnp.mean(vc * vc, axis=-1, keepdims=True)
    vn = (vc * lax.rsqrt(var + EPS) * lnw_ref[...] + lnb_ref[...]).astype(BF16)
    u = _gelu_tanh(_dot(h, wi_ref[:, 0:w]))
    ug = u * _silu(_dot(h, wi_ref[:, 2 * w:3 * w]))
    tm = x.shape[0]
    for c in range(tm // MLP_CHUNK):
        rs = slice(c * MLP_CHUNK, (c + 1) * MLP_CHUNK)
        for g in range(MLP_GROUPS):
            cs = slice(g * MLP_GROUP_DIM, (g + 1) * MLP_GROUP_DIM)
            sv = _dot(ws_ref[g], vn[rs, cs]) + bs_ref[:, cs]
            o_ref[rs, cs] = (ug[rs, cs] * sv).astype(BF16)
    y_ref[...] = x + mod[2:3] * _dot(o_ref[...], wo_ref[...])


def _mlp_layer(x, mod, nw, w_in, ln_w, ln_b, w_s, b_s_cols, w_out, tm, blocks_per_batch, side=()):
    t = x.shape[0]
    tok = pl.BlockSpec((tm, D_MODEL), lambda i: (i, 0))
    side_in, side_out, side_shape = _side_specs(side, t // tm)
    return pl.pallas_call(
        functools.partial(_mlp_kernel, n_side=len(side)),
        grid=(t // tm,),
        in_specs=[tok,
                  pl.BlockSpec((1, 3, D_MODEL), lambda i: (i // blocks_per_batch, 0, 0)),
                  _const_spec((1, D_MODEL)),
                  _const_spec((D_MODEL, 3 * MLP_WIDTH)),
                  _const_spec((1, MLP_WIDTH)),
                  _const_spec((1, MLP_WIDTH)),
                  _const_spec((MLP_GROUPS, MLP_CHUNK, MLP_CHUNK)),
                  _const_spec((MLP_CHUNK, MLP_WIDTH)),
                  _const_spec((MLP_WIDTH, D_MODEL))] + side_in,
        out_specs=[tok] + side_out,
        out_shape=[jax.ShapeDtypeStruct((t, D_MODEL), F32)] + side_shape,
        scratch_shapes=[pltpu.VMEM((tm, MLP_WIDTH), BF16)],
        compiler_params=_params(1),
        name="mlp_layer",
    )(x, mod, nw, w_in, ln_w, ln_b, w_s, b_s_cols, w_out, *(w for w, _ in side))


def kernel(x_prompt, x_sample, cache_na_k, cache_na_v, state_ret, c, c_ctx, norm_w, w_ada, b_ada,
           na_w_in, na_w_out, na_q_gain, na_k_gain, na_rpb,
           ret_w_in, ret_w_out, ret_decay_logit, ret_gn_w,
           mlp_w_in, mlp_ln_w, mlp_ln_b, mlp_w_s, mlp_b_s, mlp_w_out):
    batch, seq, _ = x_prompt.shape
    dec_batch, dec_seq, _ = x_sample.shape
    past = cache_na_k.shape[2]
    assert seq == TOKEN_BLOCK and dec_seq == 16 * GRID_W and dec_batch <= 7

    c_rows = jnp.zeros((8, D_MODEL), F32).at[:dec_batch].set(c).at[dec_batch].set(c_ctx)
    mods = _ada_all(c_rows, w_ada, b_ada).reshape(DEPTH, 8, 3, D_MODEL)

    rope_tabs = _rope_tables(dec_seq)
    bias_tiles = _na_bias_tiles(na_rpb)

    def proj_weights(i):
        j = i // N_MIXERS
        w_in, w_out = ((na_w_in, na_w_out), (ret_w_in, ret_w_out), (mlp_w_in, mlp_w_out))[i % N_MIXERS]
        return (w_in, j), (w_out, j)

    yp = x_prompt.reshape(batch * seq, D_MODEL)
    ys = x_sample.reshape(dec_batch * dec_seq, D_MODEL)
    bpb_s = dec_seq // TOKEN_BLOCK
    n_na = (DEPTH + 2) // N_MIXERS
    new_kv, new_s = None, []
    w_in, w_out = (w[j].astype(BF16) for w, j in proj_weights(0))
    for i in range(DEPTH):
        kind, j = i % N_MIXERS, i // N_MIXERS
        mod_p = mods[i, dec_batch:dec_batch + 1]
        mod_s = mods[i, :dec_batch]
        nw = norm_w[i].reshape(1, D_MODEL)
        side = proj_weights(i + 1) if i + 1 < DEPTH else ()
        if kind == 0:
            qg = jnp.tile(na_q_gain[j], NA_HEADS).reshape(1, D_MODEL)
            kg = jnp.broadcast_to(jnp.tile(na_k_gain[j], NA_HEADS)[:, None], (D_MODEL, 128))
            yp, k_new, v_new, *w_next = _na_prompt_layer(yp, mod_p, nw, w_in, w_out, j, qg, kg, n_na, new_kv, side)
            new_kv = (k_new, v_new)
            q, k, v, g = _na_proj(ys, mod_s, nw, w_in, qg, kg, bpb_s)
            kc = cache_na_k[:, j].transpose(0, 2, 3, 1).reshape(dec_batch, D_MODEL, past).astype(BF16)
            vc = cache_na_v[:, j].transpose(0, 2, 3, 1).astype(BF16)
            vc = jnp.concatenate([vc.reshape(dec_batch, NA_HEADS // 2, NA_PAIR, past),
                                  jnp.ones((dec_batch, NA_HEADS // 2, NA_PAIR, past), BF16)],
                                 axis=2).reshape(dec_batch, 2 * D_MODEL, past)
            ys = _na_latent_attn(q, k, v.reshape(dec_batch, dec_seq, 2 * D_MODEL), kc, vc,
                                 _na_bias_tables(bias_tiles, j, dec_seq // GRID_W), g, ys, mod_s, w_out)
        elif kind == 1:
            lg = jax.nn.log_sigmoid(ret_decay_logit[j].astype(F32)).reshape(2 * RET_HEADS)
            gn = ret_gn_w[j].reshape(1, RET_WIDTH)
            yp, st, *w_next = _ret_prompt_layer(lg, yp, mod_p, nw, w_in, w_out, gn, side)
            new_s.append(st)
            q, k, v, g = _ret_proj(ys, mod_s, nw, w_in, bpb_s, rope_tabs)
            ys = _ret_core(lg, q, k, v, g, ys, mod_s, gn, w_out, j, dec_seq, state_ret)
        else:
            lnw = mlp_ln_w[j].reshape(1, MLP_WIDTH)
            lnb = mlp_ln_b[j].reshape(1, MLP_WIDTH)
            w_s = mlp_w_s[j].astype(BF16)
            b_cols = jnp.repeat(mlp_b_s[j].T, MLP_GROUP_DIM, axis=1)
            yp, *w_next = _mlp_layer(yp, mod_p, nw, w_in, lnw, lnb, w_s, b_cols, w_out,
                                     MLP_TOKEN_BLOCK, batch * seq // MLP_TOKEN_BLOCK, side)
            (ys,) = _mlp_layer(ys, mod_s, nw, w_in, lnw, lnb, w_s, b_cols, w_out,
                               MLP_TOKEN_BLOCK, dec_seq // MLP_TOKEN_BLOCK)
        if w_next:
            w_in, w_out = w_next
    return (yp.reshape(batch, seq, D_MODEL),
            ys.reshape(dec_batch, dec_seq, D_MODEL),
            *(t.reshape(batch, n_na, NA_HEADS, NA_HEAD_DIM, seq).transpose(0, 1, 4, 2, 3) for t in new_kv),
            jnp.stack(new_s, axis=1))
```

```python
import functools
import math

import numpy as np
import jax
import jax.numpy as jnp
from jax import lax
from jax.experimental import pallas as pl
from jax.experimental.pallas import tpu as pltpu

F32 = jnp.float32
BF16 = jnp.bfloat16

D_MODEL = 1024
DEPTH = 4
N_MIXERS = 3
GRID_W = 64
EPS = 1e-6
NEG_INF = -1e30
LOG2E = 1.4426950408889634

NA_HEADS = 16
NA_HEAD_DIM = 64
NA_PAIR = 2 * NA_HEAD_DIM
NA_KH = 8
NA_KW = 16
NA_Q_ROWS = 4
NA_WIN_ROWS = 12
NA_TQ = NA_Q_ROWS * GRID_W
NA_WIN = NA_WIN_ROWS * GRID_W
NA_EDGE_SLABS = 2
NA_BIAS_HEADS = 4

RET_HEADS = 4
RET_QK_DIM = 256
RET_V_DIM = 512
RET_QK_WIDTH = RET_HEADS * RET_QK_DIM
RET_WIDTH = RET_HEADS * RET_V_DIM
ROPE_BASE = 10000.0

MLP_WIDTH = 2048
MLP_GROUPS = 8
MLP_GROUP_DIM = 256
MLP_CHUNK = 128

TOKEN_BLOCK = 256
MLP_TOKEN_BLOCK = 512
V7X_VMEM_LIMIT = 56 * 1024 * 1024

_NT = (((1,), (1,)), ((), ()))


def _dot(a, b):
    return jnp.dot(a, b, preferred_element_type=F32)


def _dot_nt(a, b):
    return lax.dot_general(a, b, _NT, preferred_element_type=F32)


def _silu(x):
    return x * jax.nn.sigmoid(x)


def _gelu_tanh(x):
    c = math.sqrt(2.0 / math.pi)
    return x * (0.5 * (1.0 + jnp.tanh(c * (x + 0.044715 * (x * x * x)))))


def _norm_mod(x, nw, mod):
    ms = jnp.mean(x * x, axis=-1, keepdims=True)
    y = x * lax.rsqrt(ms + EPS) * nw
    return y * (1.0 + mod[1:2]) + mod[0:1]


def _const_spec(shape):
    nd = len(shape)
    return pl.BlockSpec(shape, lambda *_: (0,) * nd, pipeline_mode=pl.Buffered(1))


def _side_specs(side, n_steps):
    in_specs, out_specs, out_shape = [], [], []
    for w, j in side:
        _, rows, cols = w.shape
        in_specs.append(pl.BlockSpec((None, rows // n_steps, cols), lambda i, j=j: (j, i, 0)))
        out_specs.append(pl.BlockSpec((rows // n_steps, cols), lambda i: (i, 0)))
        out_shape.append(jax.ShapeDtypeStruct((rows, cols), BF16))
    return in_specs, out_specs, out_shape


def _side_cast(side_in, side_out):
    for src, dst in zip(side_in, side_out):
        dst[...] = src[...].astype(BF16)


def _params(n_axes, vmem=V7X_VMEM_LIMIT):
    return pltpu.CompilerParams(dimension_semantics=("arbitrary",) * n_axes,
                                vmem_limit_bytes=vmem)


def _ada_kernel(c_ref, w_ref, b_ref, o_ref):
    a = _silu(c_ref[...]).astype(BF16)
    for n in range(w_ref.shape[2] // D_MODEL):
        cs = slice(n * D_MODEL, (n + 1) * D_MODEL)
        o_ref[0, :, cs] = _dot(a, w_ref[0, :, cs].astype(BF16)) + b_ref[0, :, cs]


def _ada_all(c_rows, w_ada, b_ada):
    return pl.pallas_call(
        _ada_kernel,
        grid=(DEPTH,),
        in_specs=[
            pl.BlockSpec((8, D_MODEL), lambda i: (0, 0)),
            pl.BlockSpec((1, D_MODEL, 3 * D_MODEL), lambda i: (i, 0, 0)),
            pl.BlockSpec((1, 1, 3 * D_MODEL), lambda i: (i, 0, 0)),
        ],
        out_specs=pl.BlockSpec((1, 8, 3 * D_MODEL), lambda i: (i, 0, 0)),
        out_shape=jax.ShapeDtypeStruct((DEPTH, 8, 3 * D_MODEL), F32),
        compiler_params=_params(1, 40 * 1024 * 1024),
        name="adaln",
    )(c_rows, w_ada, b_ada.reshape(DEPTH, 1, 3 * D_MODEL))


def _head_rms_t(t_t):
    slabs = []
    for h in range(NA_HEADS):
        blk = t_t[h * NA_HEAD_DIM:(h + 1) * NA_HEAD_DIM, :]
        ms = jnp.mean(blk * blk, axis=0, keepdims=True)
        slabs.append(blk * lax.rsqrt(ms + EPS))
    return jnp.concatenate(slabs, axis=0)


def _na_qkvg(x, mod, nw, w_ref, qg, kg_cols):
    h = _norm_mod(x, nw, mod).astype(BF16)
    w = D_MODEL
    qn = _head_rms_t(_dot(h, w_ref[:, 0:w]).T).T * qg
    q = (qn * (LOG2E * NA_HEAD_DIM ** -0.5)).astype(BF16)
    kn_t = _head_rms_t(_dot(h, w_ref[:, w:2 * w]).T)
    kn_t = kn_t * jnp.concatenate([kg_cols] * (kn_t.shape[1] // kg_cols.shape[1]), axis=1)
    v = _dot(h, w_ref[:, 2 * w:3 * w])
    g = _silu(_dot(h, w_ref[:, 3 * w:4 * w]))
    return q, kn_t, v, g


def _pair_masks():
    lane = lax.broadcasted_iota(jnp.int32, (1, NA_PAIR), 1)
    return lane < NA_HEAD_DIM


def _split_pair(q2, first):
    zero = jnp.zeros_like(q2)
    return jnp.where(first, q2, zero), jnp.where(first, zero, q2)


def _softmax_pv(s_parts, v_parts):
    m = functools.reduce(jnp.maximum, [jnp.max(s, axis=-1, keepdims=True) for s in s_parts])
    acc = None
    for s, (v, v_is_t) in zip(s_parts, v_parts):
        e = jnp.exp2((s - m).astype(BF16))
        o = _dot_nt(e, v) if v_is_t else _dot(e, v)
        acc = o if acc is None else acc + o
    return acc[:, :NA_PAIR] / acc[:, NA_PAIR:]


def _na_prompt_kernel(x_ref, mod_ref, nw_ref, w_ref, qg_ref, kg_ref, wo_ref, *rest, n_alias, n_side):
    rest = rest[n_alias:]
    y_ref, kt_ref, vt_ref = rest[n_side:n_side + 3]
    og_ref = rest[-1]
    _side_cast(rest[:n_side], rest[n_side + 3:2 * n_side + 3])
    x = x_ref[...]
    mod = mod_ref[0]
    q, kn_t, v, g = _na_qkvg(x, mod, nw_ref[...], w_ref, qg_ref[...], kg_ref[...])
    kt_ref[0, 0] = kn_t
    vt_ref[0, 0] = v.T
    kb_t, vb = kn_t.astype(BF16), v.astype(BF16)
    ones = jnp.ones((x.shape[0], NA_PAIR), BF16)
    first = _pair_masks()
    for p in range(NA_HEADS // 2):
        sl = slice(p * NA_PAIR, (p + 1) * NA_PAIR)
        k2_t = kb_t[sl, :]
        v2 = jnp.concatenate([vb[:, sl], ones], axis=1)
        outs = [_softmax_pv([_dot(qh, k2_t)], [(v2, False)]) for qh in _split_pair(q[:, sl], first)]
        og_ref[:, sl] = (jnp.where(first, outs[0], outs[1]) * g[:, sl]).astype(BF16)
    y_ref[...] = x + mod[2:3] * _dot(og_ref[...], wo_ref[...])


def _na_prompt_layer(x, mod, nw, w_in, w_out, j, q_gain, k_gain_cols, n_layers, cache_prev, side):
    t = x.shape[0]
    tm = TOKEN_BLOCK
    tok = pl.BlockSpec((tm, D_MODEL), lambda i: (i, 0))
    in_specs = [tok,
                _const_spec((1, 3, D_MODEL)),
                _const_spec((1, D_MODEL)),
                _const_spec((D_MODEL, 4 * D_MODEL)),
                _const_spec((1, D_MODEL)),
                _const_spec((D_MODEL, 128)),
                _const_spec((D_MODEL, D_MODEL))]
    args = [x, mod, nw, w_in, q_gain, k_gain_cols, w_out]
    cache = pl.BlockSpec((1, 1, D_MODEL, tm), lambda i: (i, j, 0, 0))
    cache_shape = jax.ShapeDtypeStruct((t // tm, n_layers, D_MODEL, tm), F32)
    aliases = {}
    if cache_prev is not None:
        aliases = {len(args): 1, len(args) + 1: 2}
        in_specs += [pl.BlockSpec(memory_space=pl.ANY)] * 2
        args += list(cache_prev)
    side_in, side_out, side_shape = _side_specs(side, t // tm)
    return pl.pallas_call(
        functools.partial(_na_prompt_kernel, n_alias=len(aliases), n_side=len(side)),
        grid=(t // tm,),
        in_specs=in_specs + side_in,
        out_specs=[tok, cache, cache] + side_out,
        out_shape=[jax.ShapeDtypeStruct((t, D_MODEL), F32), cache_shape, cache_shape] + side_shape,
        input_output_aliases=aliases,
        scratch_shapes=[pltpu.VMEM((tm, D_MODEL), BF16)],
        compiler_params=_params(1),
        name="na_prompt",
    )(*args, *(w for w, _ in side))


def _na_proj_kernel(x_ref, mod_ref, nw_ref, w_ref, qg_ref, kg_ref, q_out, kt_out, v_out, g_out):
    q, kn_t, v, g = _na_qkvg(x_ref[...], mod_ref[0], nw_ref[...], w_ref, qg_ref[...], kg_ref[...])
    q_out[...] = q
    kt_out[0, 0] = kn_t.astype(BF16)
    g_out[...] = g.astype(BF16)
    vb = v.astype(BF16)
    ones = jnp.ones((vb.shape[0], NA_PAIR), BF16)
    for p in range(NA_HEADS // 2):
        v_out[:, 2 * p * NA_PAIR:(2 * p + 1) * NA_PAIR] = vb[:, p * NA_PAIR:(p + 1) * NA_PAIR]
        v_out[:, (2 * p + 1) * NA_PAIR:(2 * p + 2) * NA_PAIR] = ones


def _na_proj(x, mod, nw, w_in, q_gain, k_gain_cols, blocks_per_batch):
    t = x.shape[0]
    tm = TOKEN_BLOCK
    bpb = blocks_per_batch
    tok = pl.BlockSpec((tm, D_MODEL), lambda i: (i, 0))
    tok_v = pl.BlockSpec((tm, 2 * D_MODEL), lambda i: (i, 0))
    return pl.pallas_call(
        _na_proj_kernel,
        grid=(t // tm,),
        in_specs=[tok,
                  pl.BlockSpec((1, 3, D_MODEL), lambda i: (i // bpb, 0, 0)),
                  _const_spec((1, D_MODEL)),
                  _const_spec((D_MODEL, 4 * D_MODEL)),
                  _const_spec((1, D_MODEL)),
                  _const_spec((D_MODEL, 128))],
        out_specs=[tok, pl.BlockSpec((1, 1, D_MODEL, tm), lambda i: (i // bpb, i % bpb, 0, 0)), tok_v, tok],
        out_shape=[jax.ShapeDtypeStruct((t, D_MODEL), BF16),
                   jax.ShapeDtypeStruct((t // tm // bpb, bpb, D_MODEL, tm), BF16),
                   jax.ShapeDtypeStruct((t, 2 * D_MODEL), BF16),
                   jax.ShapeDtypeStruct((t, D_MODEL), BF16)],
        compiler_params=_params(1),
        name="na_proj",
    )(x, mod, nw, w_in, q_gain, k_gain_cols)


def _na_latent_attn_kernel(q_ref, k_ref, v_ref, kc_ref, vc_ref, bias_ref, g_ref, x_ref, mod_ref,
                           wo_ref, y_ref, og_ref):
    i = pl.program_id(0)
    last = pl.num_programs(0) - 1
    first = _pair_masks()

    def attend(n_slab, blk0):
        width = n_slab * NA_TQ
        win0 = pl.multiple_of(blk0 * NA_TQ, NA_TQ)
        for p in range(NA_HEADS // 2):
            sl = slice(p * NA_PAIR, (p + 1) * NA_PAIR)
            sl2 = slice(2 * p * NA_PAIR, 2 * (p + 1) * NA_PAIR)
            kw_t = jnp.concatenate([k_ref[0, blk0 + c, sl, :] for c in range(n_slab)], axis=1)
            vw = v_ref[0, pl.ds(win0, width), sl2]
            kc_t, vc_t = kc_ref[0, sl, :], vc_ref[0, sl2, :]
            outs = []
            for hh, qh in enumerate(_split_pair(q_ref[:, sl], first)):
                s_loc = _dot(qh, kw_t) + bias_ref[2 * p + hh, 0, :, 0:width].astype(F32)
                outs.append(_softmax_pv([s_loc, _dot(qh, kc_t)], [(vw, False), (vc_t, True)]))
            o2 = jnp.where(first, outs[0], outs[1])
            og_ref[:, sl] = (o2 * g_ref[:, sl].astype(F32)).astype(BF16)
        y_ref[...] = x_ref[...] + mod_ref[0][2:3] * _dot(og_ref[...], wo_ref[...])

    edge = jnp.logical_or(i == 0, i == last)

    @pl.when(edge)
    def _():
        attend(NA_EDGE_SLABS, jnp.where(i == 0, 0, last - 1))

    @pl.when(jnp.logical_not(edge))
    def _():
        attend(NA_WIN // NA_TQ, i - 1)


def _na_latent_attn(q, k, v, kc, vc, bias, g, x, mod, w_out):
    nb, nq = k.shape[0], k.shape[1]
    n = nq * NA_TQ
    tok = pl.BlockSpec((NA_TQ, D_MODEL), lambda i, b: (b * nq + i, 0))
    full = pl.BlockSpec((1, nq, D_MODEL, NA_TQ), lambda i, b: (b, 0, 0, 0))
    full2 = pl.BlockSpec((1, n, 2 * D_MODEL), lambda i, b: (b, 0, 0))
    ctx = pl.BlockSpec((1, D_MODEL, kc.shape[2]), lambda i, b: (b, 0, 0))
    ctx2 = pl.BlockSpec((1, 2 * D_MODEL, kc.shape[2]), lambda i, b: (b, 0, 0))
    return pl.pallas_call(
        _na_latent_attn_kernel,
        grid=(nq, nb),
        in_specs=[tok, full, full2, ctx, ctx2,
                  pl.BlockSpec((NA_HEADS, 1, NA_TQ, NA_WIN), lambda i, b: (0, i, 0, 0)),
                  tok, tok,
                  pl.BlockSpec((1, 3, D_MODEL), lambda i, b: (b, 0, 0)),
                  _const_spec((D_MODEL, D_MODEL))],
        out_specs=tok,
        out_shape=jax.ShapeDtypeStruct((nb * n, D_MODEL), F32),
        scratch_shapes=[pltpu.VMEM((NA_TQ, D_MODEL), BF16)],
        compiler_params=_params(2),
        name="na_latent_attn",
    )(q, k, v, kc, vc, bias, g, x, mod, w_out)


def _na_window_plan(rows):
    n_blk = rows // NA_Q_ROWS
    kh = min(NA_KH, rows)
    n_chunk = NA_WIN_ROWS // 2
    a0 = np.zeros((n_blk, NA_Q_ROWS, n_chunk), np.int64)
    row_mask = np.zeros((n_blk, NA_TQ, NA_WIN), np.float32)
    rstart_of = lambda qr: min(max(qr - kh // 2, 0), rows - kh)
    for i in range(n_blk):
        lo = rstart_of(i * NA_Q_ROWS)
        hi = rstart_of((i + 1) * NA_Q_ROWS - 1) + kh - 1
        win_row0 = lo // NA_Q_ROWS * NA_Q_ROWS
        n_slab = hi // NA_Q_ROWS - lo // NA_Q_ROWS + 1
        assert n_slab == (NA_EDGE_SLABS if i in (0, n_blk - 1) else NA_WIN_ROWS // NA_Q_ROWS)
        assert win_row0 == NA_Q_ROWS * (0 if i == 0 else n_blk - 2 if i == n_blk - 1 else i - 1)
        for ql in range(NA_Q_ROWS):
            qr = i * NA_Q_ROWS + ql
            rstart = rstart_of(qr)
            for kl in range(NA_WIN_ROWS):
                kr = win_row0 + kl
                if not rstart <= kr < rstart + kh:
                    row_mask[i, ql * GRID_W:(ql + 1) * GRID_W, kl * GRID_W:(kl + 1) * GRID_W] = NEG_INF
            for m in range(n_chunk):
                a0[i, ql, m] = win_row0 + 2 * m - qr + (NA_KH - 1)
    return a0, row_mask


def _na_bias_kernel(t_ref, cm_ref, rm_ref, o_ref, *, a0):
    n_off = t_ref.shape[1]
    outside = jnp.full((GRID_W, GRID_W), NEG_INF, F32)
    n_blk, n_ql, n_chunk = a0.shape
    for h in range(t_ref.shape[0]):
        def tile(a):
            return t_ref[h, a] * LOG2E if 0 <= a < n_off else outside

        pairs = {a: jnp.concatenate([tile(a), tile(a + 1)], axis=1) + cm_ref[...]
                 for a in sorted(set(int(a) for a in a0.flatten()))}
        for i in range(n_blk):
            for ql in range(n_ql):
                rs = slice(ql * GRID_W, (ql + 1) * GRID_W)
                for m in range(n_chunk):
                    cs = slice(m * 128, (m + 1) * 128)
                    o_ref[h, i, rs, cs] = (pairs[int(a0[i, ql, m])] + rm_ref[i, rs, cs].astype(F32)).astype(BF16)


def _na_column_window():
    cq = np.arange(GRID_W)
    cstart = np.clip(cq - NA_KW // 2, 0, GRID_W - NA_KW)
    col_ok = (cq[None, :] >= cstart[:, None]) & (cq[None, :] < cstart[:, None] + NA_KW)
    coff = np.clip(cq[None, :] - cq[:, None], -(NA_KW - 1), NA_KW - 1) + (NA_KW - 1)
    return col_ok, coff


def _na_bias_tables(rpb, rows):
    a0, row_mask = _na_window_plan(rows)
    n_blk = a0.shape[0]
    n_off = 2 * NA_KH - 1
    col_ok, coff = _na_column_window()
    onehot = (coff[None] == np.arange(2 * NA_KW - 1)[:, None, None]) & col_ok[None]
    tiles = jnp.einsum("hab,bqk->haqk", rpb.astype(BF16), jnp.asarray(onehot, BF16), preferred_element_type=F32)
    col_mask = np.where(np.concatenate([col_ok, col_ok], axis=1), 0.0, NEG_INF).astype(np.float32)
    return pl.pallas_call(
        functools.partial(_na_bias_kernel, a0=a0),
        grid=(NA_HEADS // NA_BIAS_HEADS,),
        in_specs=[pl.BlockSpec((NA_BIAS_HEADS, n_off, GRID_W, GRID_W), lambda h: (h, 0, 0, 0)),
                  _const_spec((GRID_W, 128)),
                  _const_spec((n_blk, NA_TQ, NA_WIN))],
        out_specs=pl.BlockSpec((NA_BIAS_HEADS, n_blk, NA_TQ, NA_WIN), lambda h: (h, 0, 0, 0)),
        out_shape=jax.ShapeDtypeStruct((NA_HEADS, n_blk, NA_TQ, NA_WIN), BF16),
        compiler_params=_params(1, 32 * 1024 * 1024),
        name="na_bias",
    )(tiles, jnp.asarray(col_mask), jnp.asarray(row_mask, BF16))


def _ret_qkvg(x, mod, nw, w_ref, rope_refs):
    h = _norm_mod(x, nw, mod).astype(BF16)
    qw, vw = RET_QK_WIDTH, RET_WIDTH

    def rot(t):
        if rope_refs is None:
            return t
        cos, sin = rope_refs[0][...], rope_refs[1][...]
        parts = []
        for c in range(qw // 128):
            tc = t[:, c * 128:(c + 1) * 128]
            hs = slice(c * 128 % RET_QK_DIM, c * 128 % RET_QK_DIM + 128)
            parts.append(tc * cos[:, hs] + pltpu.roll(tc, 64, axis=1) * sin[:, hs])
        return jnp.concatenate(parts, axis=1)

    q = rot(_dot(h, w_ref[:, 0:qw])).astype(BF16)
    k = (rot(_dot(h, w_ref[:, qw:2 * qw])) * (RET_QK_DIM ** -0.5)).astype(BF16)
    v = _dot(h, w_ref[:, 2 * qw:2 * qw + vw]).astype(BF16)
    g = _silu(_dot(h, w_ref[:, 2 * qw + vw:2 * qw + 2 * vw]))
    return q, k, v, g


def _ret_decay_init(lg_ref, dec_ref, row0):
    _, tq, n = dec_ref.shape
    qi = (row0 + lax.broadcasted_iota(jnp.int32, (tq, n), 0)).astype(F32)
    diff = qi - lax.broadcasted_iota(jnp.int32, (tq, n), 1).astype(F32)
    for h in range(RET_HEADS):
        lgf, lgb = lg_ref[h], lg_ref[RET_HEADS + h]
        dec = jnp.exp(jnp.where(diff >= 0.0, lgf * diff, -lgb * diff))
        dec_ref[h] = jnp.where(diff == 0.0, 2.0, dec).astype(dec_ref.dtype)


def _ret_mix(lg_ref, dec_ref, q, k, v, g, gn_ref, yg_ref, row0, s0_ref, st_ref):
    _, tq, n = dec_ref.shape
    for h in range(RET_HEADS):
        lgf, lgb = lg_ref[h], lg_ref[RET_HEADS + h]
        qh = q[:, h * RET_QK_DIM:(h + 1) * RET_QK_DIM]
        kh = k[:, h * RET_QK_DIM:(h + 1) * RET_QK_DIM]
        vh = v[:, h * RET_V_DIM:(h + 1) * RET_V_DIM]
        inner = (_dot_nt(qh, kh) * dec_ref[h]).astype(BF16)
        o = _dot(inner, vh)
        if s0_ref is not None:
            pos = (row0 + lax.broadcasted_iota(jnp.int32, (tq, 1), 0)).astype(F32)
            o = o + _dot(qh, s0_ref[0, 0, 0, h].astype(BF16)) * jnp.exp(lgf * (pos + 1.0))
            o = o + _dot(qh, s0_ref[0, 0, 1, h].astype(BF16)) * jnp.exp(lgb * (float(n) - pos))
        if st_ref is not None:
            kpos = lax.broadcasted_iota(jnp.int32, (n, RET_QK_DIM), 0).astype(F32)
            kf = kh.astype(F32)
            kdf = (kf * jnp.exp(lgf * (float(n - 1) - kpos))).T.astype(BF16)
            kdb = (kf * jnp.exp(lgb * kpos)).T.astype(BF16)
            st_ref[0, 0, h] = _dot(kdf, vh)
            st_ref[0, 1, h] = _dot(kdb, vh)
        mu = jnp.mean(o, axis=-1, keepdims=True)
        oc = o - mu
        var = jnp.mean(oc * oc, axis=-1, keepdims=True)
        vs = slice(h * RET_V_DIM, (h + 1) * RET_V_DIM)
        yn = oc * lax.rsqrt(var + EPS) * gn_ref[:, vs]
        yg_ref[:, vs] = (yn * g[:, vs].astype(F32)).astype(BF16)


def _ret_prompt_kernel(lg_ref, x_ref, mod_ref, nw_ref, w_ref, gn_ref, wo_ref, *rest, n_side):
    y_ref, st_ref = rest[n_side:n_side + 2]
    dec_ref, yg_ref = rest[-2:]
    _side_cast(rest[:n_side], rest[n_side + 2:2 * n_side + 2])

    @pl.when(pl.program_id(0) == 0)
    def _():
        _ret_decay_init(lg_ref, dec_ref, 0)

    x = x_ref[...]
    mod = mod_ref[0]
    q, k, v, g = _ret_qkvg(x, mod, nw_ref[...], w_ref, None)
    _ret_mix(lg_ref, dec_ref, q, k, v, g, gn_ref, yg_ref, 0, None, st_ref)
    y_ref[...] = x + mod[2:3] * _dot(yg_ref[...], wo_ref[...])


def _ret_prompt_layer(lg, x, mod, nw, w_in, w_out, gn_w, side):
    t = x.shape[0]
    n = TOKEN_BLOCK
    tok = pl.BlockSpec((n, D_MODEL), lambda b: (b, 0))
    st_block = (1, 2, RET_HEADS, RET_QK_DIM, RET_V_DIM)
    side_in, side_out, side_shape = _side_specs(side, t // n)
    return pl.pallas_call(
        functools.partial(_ret_prompt_kernel, n_side=len(side)),
        grid=(t // n,),
        in_specs=[pl.BlockSpec(memory_space=pltpu.SMEM),
                  tok,
                  _const_spec((1, 3, D_MODEL)),
                  _const_spec((1, D_MODEL)),
                  _const_spec((D_MODEL, 2 * RET_QK_WIDTH + 2 * RET_WIDTH)),
                  _const_spec((1, RET_WIDTH)),
                  _const_spec((RET_WIDTH, D_MODEL))] + side_in,
        out_specs=[tok, pl.BlockSpec(st_block, lambda b: (b, 0, 0, 0, 0))] + side_out,
        out_shape=[jax.ShapeDtypeStruct((t, D_MODEL), F32),
                   jax.ShapeDtypeStruct((t // n,) + st_block[1:], F32)] + side_shape,
        scratch_shapes=[pltpu.VMEM((RET_HEADS, n, n), F32), pltpu.VMEM((n, RET_WIDTH), BF16)],
        compiler_params=_params(1),
        name="ret_prompt",
    )(lg, x, mod, nw, w_in, gn_w, w_out, *(w for w, _ in side))


def _ret_proj_kernel(x_ref, mod_ref, nw_ref, w_ref, cos_ref, sin_ref, q_out, k_out, v_out, g_out):
    q, k, v, g = _ret_qkvg(x_ref[...], mod_ref[0], nw_ref[...], w_ref, (cos_ref, sin_ref))
    q_out[...] = q
    k_out[...] = k
    v_out[...] = v
    g_out[...] = g.astype(BF16)


def _ret_proj(x, mod, nw, w_in, blocks_per_batch, rope_tables):
    t = x.shape[0]
    tm = TOKEN_BLOCK
    tok = pl.BlockSpec((tm, D_MODEL), lambda i: (i, 0))
    tok2 = pl.BlockSpec((tm, RET_WIDTH), lambda i: (i, 0))
    pos = pl.BlockSpec((tm, RET_QK_DIM), lambda i: (i % blocks_per_batch, 0))
    return pl.pallas_call(
        _ret_proj_kernel,
        grid=(t // tm,),
        in_specs=[tok,
                  pl.BlockSpec((1, 3, D_MODEL), lambda i: (i // blocks_per_batch, 0, 0)),
                  _const_spec((1, D_MODEL)),
                  _const_spec((D_MODEL, 2 * RET_QK_WIDTH + 2 * RET_WIDTH)),
                  pos, pos],
        out_specs=[tok, tok, tok2, tok2],
        out_shape=[jax.ShapeDtypeStruct((t, RET_QK_WIDTH), BF16)] * 2
        + [jax.ShapeDtypeStruct((t, RET_WIDTH), BF16)] * 2,
        compiler_params=_params(1),
        name="ret_proj",
    )(x, mod, nw, w_in, *rope_tables)


def _rope_tables(n):
    half = RET_QK_DIM // 2
    t = np.arange(n)
    inv = ROPE_BASE ** (-jnp.arange(0, half, 2, dtype=F32) / half)
    tabs = []
    for pos in ((t // GRID_W).astype(np.float32), (t % GRID_W).astype(np.float32)):
        ang = jnp.asarray(pos)[:, None] * inv[None, :]
        tabs.append((jnp.cos(ang), jnp.sin(ang)))
    cos = jnp.concatenate([tabs[0][0], tabs[0][0], tabs[1][0], tabs[1][0]], axis=1)
    sin = jnp.concatenate([-tabs[0][1], tabs[0][1], -tabs[1][1], tabs[1][1]], axis=1)
    return cos, sin


def _ret_core_kernel(lg_ref, q_ref, k_ref, v_ref, g_ref, x_ref, mod_ref, gn_ref, wo_ref, s0_ref,
                     y_ref, dec_ref, yg_ref):
    t = pl.program_id(1)
    row0 = t * q_ref.shape[0]
    dec_t = dec_ref.at[t]

    @pl.when(pl.program_id(0) == 0)
    def _():
        _ret_decay_init(lg_ref, dec_t, row0)

    _ret_mix(lg_ref, dec_t, q_ref, k_ref.at[0], v_ref.at[0], g_ref, gn_ref, yg_ref, row0, s0_ref, None)
    y_ref[...] = x_ref[...] + mod_ref[0][2:3] * _dot(yg_ref[...], wo_ref[...])


def _ret_core(lg, q, k, v, g, x, mod, gn_w, w_out, j, n, s0):
    t = x.shape[0]
    nb = t // n
    tq = TOKEN_BLOCK
    nq = n // tq
    tok = pl.BlockSpec((tq, D_MODEL), lambda b, i: (b * nq + i, 0))
    tok2 = pl.BlockSpec((tq, RET_WIDTH), lambda b, i: (b * nq + i, 0))
    st_block = (1, 1, 2, RET_HEADS, RET_QK_DIM, RET_V_DIM)
    return pl.pallas_call(
        _ret_core_kernel,
        grid=(nb, nq),
        in_specs=[pl.BlockSpec(memory_space=pltpu.SMEM),
                  tok,
                  pl.BlockSpec((1, n, RET_QK_WIDTH), lambda b, i: (b, 0, 0)),
                  pl.BlockSpec((1, n, RET_WIDTH), lambda b, i: (b, 0, 0)),
                  tok2, tok,
                  pl.BlockSpec((1, 3, D_MODEL), lambda b, i: (b, 0, 0)),
                  _const_spec((1, RET_WIDTH)),
                  _const_spec((RET_WIDTH, D_MODEL)),
                  pl.BlockSpec(st_block, lambda b, i: (b, j, 0, 0, 0, 0))],
        out_specs=tok,
        out_shape=jax.ShapeDtypeStruct((t, D_MODEL), F32),
        scratch_shapes=[pltpu.VMEM((nq, RET_HEADS, tq, n), BF16), pltpu.VMEM((tq, RET_WIDTH), BF16)],
        compiler_params=_params(2),
        name="ret_core",
    )(lg, q, k.reshape(nb, n, RET_QK_WIDTH), v.reshape(nb, n, RET_WIDTH), g, x, mod, gn_w, w_out, s0)


def _mlp_kernel(x_ref, mod_ref, nw_ref, wi_ref, lnw_ref, lnb_ref, ws_ref, bs_ref, wo_ref, *rest, n_side):
    y_ref, o_ref = rest[n_side], rest[-1]
    _side_cast(rest[:n_side], rest[n_side + 1:2 * n_side + 1])
    x = x_ref[...]
    mod = mod_ref[0]
    h = _norm_mod(x, nw_ref[...], mod).astype(BF16)
    w = MLP_WIDTH
    v = _gelu_tanh(_dot(h, wi_ref[:, w:2 * w]))
    mu = jnp.mean(v, axis=-1, keepdims=True)
    vc = v - mu
    var = jnp.mean(vc * vc, axis=-1, keepdims=True)
    vn = (vc * lax.rsqrt(var + EPS) * lnw_ref[...] + lnb_ref[...]).astype(BF16)
    u = _gelu_tanh(_dot(h, wi_ref[:, 0:w]))
    ug = u * _silu(_dot(h, wi_ref[:, 2 * w:3 * w]))
    tm = x.shape[0]
    for c in range(tm // MLP_CHUNK):
        rs = slice(c * MLP_CHUNK, (c + 1) * MLP_CHUNK)
        for g in range(MLP_GROUPS):
            cs = slice(g * MLP_GROUP_DIM, (g + 1) * MLP_GROUP_DIM)
            sv = _dot(ws_ref[g], vn[rs, cs]) + bs_ref[:, cs]
            o_ref[rs, cs] = (ug[rs, cs] * sv).astype(BF16)
    y_ref[...] = x + mod[2:3] * _dot(o_ref[...], wo_ref[...])


def _mlp_layer(x, mod, nw, w_in, ln_w, ln_b, w_s, b_s_cols, w_out, tm, blocks_per_batch, side=()):
    t = x.shape[0]
    tok = pl.BlockSpec((tm, D_MODEL), lambda i: (i, 0))
    side_in, side_out, side_shape = _side_specs(side, t // tm)
    return pl.pallas_call(
        functools.partial(_mlp_kernel, n_side=len(side)),
        grid=(t // tm,),
        in_specs=[tok,
                  pl.BlockSpec((1, 3, D_MODEL), lambda i: (i // blocks_per_batch, 0, 0)),
                  _const_spec((1, D_MODEL)),
                  _const_spec((D_MODEL, 3 * MLP_WIDTH)),
                  _const_spec((1, MLP_WIDTH)),
                  _const_spec((1, MLP_WIDTH)),
                  _const_spec((MLP_GROUPS, MLP_CHUNK, MLP_CHUNK)),
                  _const_spec((MLP_CHUNK, MLP_WIDTH)),
                  _const_spec((MLP_WIDTH, D_MODEL))] + side_in,
        out_specs=[tok] + side_out,
        out_shape=[jax.ShapeDtypeStruct((t, D_MODEL), F32)] + side_shape,
        scratch_shapes=[pltpu.VMEM((tm, MLP_WIDTH), BF16)],
        compiler_params=_params(1),
        name="mlp_layer",
    )(x, mod, nw, w_in, ln_w, ln_b, w_s, b_s_cols, w_out, *(w for w, _ in side))


def kernel(x_prompt, x_sample, cache_na_k, cache_na_v, state_ret, c, c_ctx, norm_w, w_ada, b_ada,
           na_w_in, na_w_out, na_q_gain, na_k_gain, na_rpb,
           ret_w_in, ret_w_out, ret_decay_logit, ret_gn_w,
           mlp_w_in, mlp_ln_w, mlp_ln_b, mlp_w_s, mlp_b_s, mlp_w_out):
    batch, seq, _ = x_prompt.shape
    dec_batch, dec_seq, _ = x_sample.shape
    past = cache_na_k.shape[2]
    assert seq == TOKEN_BLOCK and dec_seq == 16 * GRID_W and dec_batch <= 7

    c_rows = jnp.zeros((8, D_MODEL), F32).at[:dec_batch].set(c).at[dec_batch].set(c_ctx)
    mods = _ada_all(c_rows, w_ada, b_ada).reshape(DEPTH, 8, 3, D_MODEL)

    rope_tabs = _rope_tables(dec_seq)

    def proj_weights(i):
        j = i // N_MIXERS
        w_in, w_out = ((na_w_in, na_w_out), (ret_w_in, ret_w_out), (mlp_w_in, mlp_w_out))[i % N_MIXERS]
        return (w_in, j), (w_out, j)

    yp = x_prompt.reshape(batch * seq, D_MODEL)
    ys = x_sample.reshape(dec_batch * dec_seq, D_MODEL)
    bpb_s = dec_seq // TOKEN_BLOCK
    n_na = (DEPTH + 2) // N_MIXERS
    new_kv, new_s = None, []
    w_in, w_out = (w[j].astype(BF16) for w, j in proj_weights(0))
    for i in range(DEPTH):
        kind, j = i % N_MIXERS, i // N_MIXERS
        mod_p = mods[i, dec_batch:dec_batch + 1]
        mod_s = mods[i, :dec_batch]
        nw = norm_w[i].reshape(1, D_MODEL)
        side = proj_weights(i + 1) if i + 1 < DEPTH else ()
        if kind == 0:
            qg = jnp.tile(na_q_gain[j], NA_HEADS).reshape(1, D_MODEL)
            kg = jnp.broadcast_to(jnp.tile(na_k_gain[j], NA_HEADS)[:, None], (D_MODEL, 128))
            yp, k_new, v_new, *w_next = _na_prompt_layer(yp, mod_p, nw, w_in, w_out, j, qg, kg, n_na, new_kv, side)
            new_kv = (k_new, v_new)
            q, k, v, g = _na_proj(ys, mod_s, nw, w_in, qg, kg, bpb_s)
            kc = cache_na_k[:, j].transpose(0, 2, 3, 1).reshape(dec_batch, D_MODEL, past).astype(BF16)
            vc = cache_na_v[:, j].transpose(0, 2, 3, 1).astype(BF16)
            vc = jnp.concatenate([vc.reshape(dec_batch, NA_HEADS // 2, NA_PAIR, past),
                                  jnp.ones((dec_batch, NA_HEADS // 2, NA_PAIR, past), BF16)],
                                 axis=2).reshape(dec_batch, 2 * D_MODEL, past)
            ys = _na_latent_attn(q, k, v.reshape(dec_batch, dec_seq, 2 * D_MODEL), kc, vc,
                                 _na_bias_tables(na_rpb[j], dec_seq // GRID_W), g, ys, mod_s, w_out)
        elif kind == 1:
            lg = jax.nn.log_sigmoid(ret_decay_logit[j].astype(F32)).reshape(2 * RET_HEADS)
            gn = ret_gn_w[j].reshape(1, RET_WIDTH)
            yp, st, *w_next = _ret_prompt_layer(lg, yp, mod_p, nw, w_in, w_out, gn, side)
            new_s.append(st)
            q, k, v, g = _ret_proj(ys, mod_s, nw, w_in, bpb_s, rope_tabs)
            ys = _ret_core(lg, q, k, v, g, ys, mod_s, gn, w_out, j, dec_seq, state_ret)
        else:
            lnw = mlp_ln_w[j].reshape(1, MLP_WIDTH)
            lnb = mlp_ln_b[j].reshape(1, MLP_WIDTH)
            w_s = mlp_w_s[j].astype(BF16)
            b_cols = jnp.repeat(mlp_b_s[j].T, MLP_GROUP_DIM, axis=1)
            yp, *w_next = _mlp_layer(yp, mod_p, nw, w_in, lnw, lnb, w_s, b_cols, w_out,
                                     MLP_TOKEN_BLOCK, batch * seq // MLP_TOKEN_BLOCK, side)
            (ys,) = _mlp_layer(ys, mod_s, nw, w_in, lnw, lnb, w_s, b_cols, w_out,
                               MLP_TOKEN_BLOCK, dec_seq // MLP_TOKEN_BLOCK)
        if w_next:
            w_in, w_out = w_next
    return (yp.reshape(batch, seq, D_MODEL),
            ys.reshape(dec_batch, dec_seq, D_MODEL),
            *(t.reshape(batch, n_na, NA_HEADS, NA_HEAD_DIM, seq).transpose(0, 1, 4, 2, 3) for t in new_kv),
            jnp.stack(new_s, axis=1))
```
